```python
import math
import jax
import jax.numpy as jnp
from jax import lax

D_MODEL = 1024
BATCH = 16
SEQ = 4096
DEPTH = 1

D_FF = 2816
FFN_RESIDUAL_WEIGHT = 0.5
SSM_D_INNER = 2 * D_MODEL
SSM_HEAD_DIM = 64
SSM_HEADS = SSM_D_INNER // SSM_HEAD_DIM
SSM_GROUPS = 4
SSM_STATE = 128
SSM_CONV = 4
SSM_CHUNK = 128
SSM_CONV_DIM = SSM_D_INNER + 2 * SSM_GROUPS * SSM_STATE
ATTN_Q_HEADS = 16
ATTN_KV_HEADS = 4
ATTN_HEAD_DIM = 64
ATTN_WINDOW = 128
ATTN_BLOCK = 128
ATTN_Q_DIM = ATTN_Q_HEADS * ATTN_HEAD_DIM
ATTN_KV_DIM = ATTN_KV_HEADS * ATTN_HEAD_DIM
REL_BUCKETS = 32
REL_MAX_DISTANCE = 128
RMS_EPS = 1e-6
IN_SPLITS = (D_MODEL, D_MODEL, SSM_D_INNER, SSM_CONV_DIM, SSM_HEADS, ATTN_Q_DIM, ATTN_KV_DIM, ATTN_KV_DIM)
IN_COLS = D_MODEL + D_MODEL + SSM_D_INNER + SSM_CONV_DIM + SSM_HEADS + ATTN_Q_DIM + 2 * ATTN_KV_DIM

kernel_name = 'hybrid_ssd_swa_sink_macaron_block'


def rmsnorm(x, g):
    xf = x.astype(jnp.float32)
    y = xf * lax.rsqrt(jnp.mean(xf * xf, axis=-1, keepdims=True) + RMS_EPS)
    return (y * g.astype(jnp.float32)).astype(x.dtype)


def swiglu(x, w_gate, w_up, w_down):
    return (jax.nn.silu(x @ w_gate) * (x @ w_up)) @ w_down


def t5_causal_bucket(dist):
    max_exact = REL_BUCKETS // 2
    d = jnp.maximum(dist, 1).astype(jnp.float32)
    large = max_exact + (jnp.log(d / max_exact) / math.log(REL_MAX_DISTANCE / max_exact)
                         * (REL_BUCKETS - max_exact)).astype(jnp.int32)
    large = jnp.minimum(large, REL_BUCKETS - 1)
    return jnp.where(dist < max_exact, dist, large)


def causal_depthwise_conv(x, w, bias):
    y = lax.conv_general_dilated(
        x, w[:, None, :].astype(x.dtype), window_strides=(1,), padding=[(SSM_CONV - 1, 0)],
        dimension_numbers=('NWC', 'WIO', 'NWC'), feature_group_count=x.shape[-1])
    return y + bias.astype(x.dtype)


def ssd_chunked_scan(xs, dt, a, bm, cm):
    b, s = xs.shape[:2]
    nc = s // SSM_CHUNK
    q = SSM_CHUNK
    r = SSM_HEADS // SSM_GROUPS
    f32 = jnp.float32
    x = (xs.astype(f32) * dt[..., None]).reshape(b, nc, q, SSM_GROUPS, r, SSM_HEAD_DIM)
    a_cs = jnp.cumsum((dt * a).reshape(b, nc, q, SSM_GROUPS, r), axis=2)
    bc = bm.astype(f32).reshape(b, nc, q, SSM_GROUPS, SSM_STATE)
    cc = cm.astype(f32).reshape(b, nc, q, SSM_GROUPS, SSM_STATE)
    causal = jnp.tril(jnp.ones((q, q), dtype=bool))[:, :, None, None]
    seg = a_cs[:, :, :, None] - a_cs[:, :, None]
    decay = jnp.exp(jnp.where(causal, seg, -jnp.inf))
    scores = jnp.einsum('bcign,bcjgn->bcijg', cc, bc)
    y_diag = jnp.einsum('bcijgr,bcjgrp->bcigrp', scores[..., None] * decay, x)
    x_w = x * jnp.exp(a_cs[:, :, -1:] - a_cs)[..., None]
    states = jnp.einsum('bcjgn,bcjgrp->bcgrpn', bc, x_w)
    chunk_decay = jnp.exp(a_cs[:, :, -1])

    def step(h, inp):
        s_c, d_c = inp
        return h * d_c[..., None, None] + s_c, h

    h0 = jnp.zeros((b, SSM_GROUPS, r, SSM_HEAD_DIM, SSM_STATE), f32)
    _, prev = lax.scan(step, h0, (jnp.moveaxis(states, 1, 0), jnp.moveaxis(chunk_decay, 1, 0)))
    prev = jnp.moveaxis(prev, 0, 1)
    y_off = jnp.einsum('bcign,bcgrpn->bcigrp', cc, prev) * jnp.exp(a_cs)[..., None]
    return (y_diag + y_off).reshape(b, s, SSM_HEADS, SSM_HEAD_DIM)


def ssd_branch(z, xbc, dt_raw, conv_w, conv_b, dt_bias, a_log, d_skip, norm_g):
    b, s = z.shape[:2]
    f32 = jnp.float32
    xbc = jax.nn.silu(causal_depthwise_conv(xbc, conv_w, conv_b))
    xs = xbc[..., :SSM_D_INNER].reshape(b, s, SSM_HEADS, SSM_HEAD_DIM)
    bm = xbc[..., SSM_D_INNER:SSM_D_INNER + SSM_GROUPS * SSM_STATE].reshape(b, s, SSM_GROUPS, SSM_STATE)
    cm = xbc[..., SSM_D_INNER + SSM_GROUPS * SSM_STATE:].reshape(b, s, SSM_GROUPS, SSM_STATE)
    dt = jax.nn.softplus(dt_raw.astype(f32) + dt_bias.astype(f32))
    a = -jnp.exp(a_log.astype(f32))
    y = ssd_chunked_scan(xs, dt, a, bm, cm)
    y = y + d_skip.astype(f32)[:, None] * xs.astype(f32)
    yg = (y.reshape(b, s, SSM_D_INNER) * jax.nn.silu(z.astype(f32))).reshape(b, s, SSM_GROUPS, -1)
    yg = yg * lax.rsqrt(jnp.mean(yg * yg, axis=-1, keepdims=True) + RMS_EPS)
    return (yg.reshape(b, s, SSM_D_INNER) * norm_g.astype(f32)).astype(z.dtype)


def swa_sink_attention(q, k, v, sinks, rel_table):
    b, s = q.shape[:2]
    nb = s // ATTN_BLOCK
    r = ATTN_Q_HEADS // ATTN_KV_HEADS
    f32 = jnp.float32
    blk = ATTN_BLOCK
    qb = q.astype(f32).reshape(b, nb, blk, ATTN_KV_HEADS, r, ATTN_HEAD_DIM) * (ATTN_HEAD_DIM ** -0.5)
    kb = k.astype(f32).reshape(b, nb, blk, ATTN_KV_HEADS, ATTN_HEAD_DIM)
    vb = v.astype(f32).reshape(b, nb, blk, ATTN_KV_HEADS, ATTN_HEAD_DIM)

    def band(t):
        prev = jnp.concatenate([jnp.zeros_like(t[:, :1]), t[:, :-1]], axis=1)
        return jnp.concatenate([prev, t], axis=2)

    kk, vv = band(kb), band(vb)
    qi = jnp.arange(blk)[:, None]
    kj = jnp.arange(2 * blk)[None, :]
    dist = qi + blk - kj
    in_window = (dist >= 0) & (dist < ATTN_WINDOW)
    key_exists = (jnp.arange(nb)[:, None, None] > 0) | (kj >= blk)[None]
    mask = in_window[None] & key_exists
    bias = rel_table.astype(f32)[t5_causal_bucket(jnp.maximum(dist, 0))]
    bias = jnp.transpose(bias, (2, 0, 1)).reshape(ATTN_KV_HEADS, r, 1, blk, 2 * blk)
    logits = jnp.einsum('bnikrd,bnjkd->bkrnij', qb, kk) + bias
    logits = jnp.where(mask, logits, -jnp.inf)
    sink = sinks.astype(f32).reshape(ATTN_KV_HEADS, r, 1, 1)
    m = jnp.maximum(logits.max(axis=-1), sink)
    p = jnp.exp(logits - m[..., None])
    p = p / (p.sum(axis=-1) + jnp.exp(sink - m))[..., None]
    o = jnp.einsum('bkrnij,bnjkd->bnikrd', p, vv)
    return o.reshape(b, s, ATTN_Q_DIM).astype(q.dtype)


def hybrid_mixer(u, w_in, conv_w, conv_b, dt_bias, a_log, d_skip, ssm_norm_g, w_ssm_proj,
                 attn_sinks, rel_table, w_attn_proj, w_out):
    proj = u @ w_in
    parts = []
    start = 0
    for size in IN_SPLITS:
        parts.append(proj[..., start:start + size])
        start += size
    g_ssm, g_attn, z, xbc, dt_raw, q, k, v = parts
    y_ssm = ssd_branch(z, xbc, dt_raw, conv_w, conv_b, dt_bias, a_log, d_skip, ssm_norm_g) @ w_ssm_proj
    y_attn = swa_sink_attention(q, k, v, attn_sinks, rel_table) @ w_attn_proj
    merged = jax.nn.sigmoid(g_ssm) * y_ssm + jax.nn.sigmoid(g_attn) * y_attn
    return merged @ w_out


def setup_inputs(seed: int = 0) -> dict:
    key = jax.random.key(seed)
    ks = jax.random.split(key, 32)
    f32 = jnp.float32

    def dense(k, fan_in, fan_out):
        return jax.random.normal(k, (DEPTH, fan_in, fan_out), f32) * fan_in ** -0.5

    def gain(k, n):
        return 1.0 + 0.05 * jax.random.normal(k, (DEPTH, n), f32)

    dt0 = jnp.exp(jax.random.uniform(ks[20], (DEPTH, SSM_HEADS), f32, math.log(1e-3), math.log(1e-1)))
    return {
        'x': jax.random.normal(ks[0], (BATCH, SEQ, D_MODEL), f32),
        'ffn1_pre_g': gain(ks[1], D_MODEL),
        'ffn1_w_gate': dense(ks[2], D_MODEL, D_FF),
        'ffn1_w_up': dense(ks[3], D_MODEL, D_FF),
        'ffn1_w_down': dense(ks[4], D_FF, D_MODEL),
        'ffn1_post_g': gain(ks[5], D_MODEL),
        'mix_pre_g': gain(ks[6], D_MODEL),
        'w_in': dense(ks[7], D_MODEL, IN_COLS),
        'conv_w': jax.random.normal(ks[8], (DEPTH, SSM_CONV, SSM_CONV_DIM), f32) * SSM_CONV ** -0.5,
        'conv_b': 0.02 * jax.random.normal(ks[9], (DEPTH, SSM_CONV_DIM), f32),
        'dt_bias': dt0 + jnp.log(-jnp.expm1(-dt0)),
        'a_log': jnp.log(jax.random.uniform(ks[10], (DEPTH, SSM_HEADS), f32, 1.0, 16.0)),
        'd_skip': 1.0 + 0.1 * jax.random.normal(ks[11], (DEPTH, SSM_HEADS), f32),
        'ssm_norm_g': gain(ks[12], SSM_D_INNER),
        'w_ssm_proj': dense(ks[13], SSM_D_INNER, D_MODEL),
        'attn_sinks': 0.5 * jax.random.normal(ks[14], (DEPTH, ATTN_Q_HEADS), f32),
        'rel_bias_table': 0.5 * jax.random.normal(ks[15], (REL_BUCKETS, ATTN_Q_HEADS), f32),
        'w_attn_proj': dense(ks[16], ATTN_Q_DIM, D_MODEL),
        'w_out': dense(ks[17], D_MODEL, D_MODEL),
        'mix_post_g': gain(ks[18], D_MODEL),
        'ffn2_pre_g': gain(ks[19], D_MODEL),
        'ffn2_w_gate': dense(ks[21], D_MODEL, D_FF),
        'ffn2_w_up': dense(ks[22], D_MODEL, D_FF),
        'ffn2_w_down': dense(ks[23], D_FF, D_MODEL),
        'ffn2_post_g': gain(ks[24], D_MODEL),
    }


def reference(x, ffn1_pre_g, ffn1_w_gate, ffn1_w_up, ffn1_w_down, ffn1_post_g, mix_pre_g, w_in,
              conv_w, conv_b, dt_bias, a_log, d_skip, ssm_norm_g, w_ssm_proj, attn_sinks,
              rel_bias_table, w_attn_proj, w_out, mix_post_g, ffn2_pre_g, ffn2_w_gate, ffn2_w_up,
              ffn2_w_down, ffn2_post_g):
    h = x
    for l in range(DEPTH):
        f1 = swiglu(rmsnorm(h, ffn1_pre_g[l]), ffn1_w_gate[l], ffn1_w_up[l], ffn1_w_down[l])
        h = h + FFN_RESIDUAL_WEIGHT * rmsnorm(f1, ffn1_post_g[l])
        mix = hybrid_mixer(rmsnorm(h, mix_pre_g[l]), w_in[l], conv_w[l], conv_b[l], dt_bias[l],
                           a_log[l], d_skip[l], ssm_norm_g[l], w_ssm_proj[l], attn_sinks[l],
                           rel_bias_table, w_attn_proj[l], w_out[l])
        h = h + rmsnorm(mix, mix_post_g[l])
        f2 = swiglu(rmsnorm(h, ffn2_pre_g[l]), ffn2_w_gate[l], ffn2_w_up[l], ffn2_w_down[l])
        h = h + FFN_RESIDUAL_WEIGHT * rmsnorm(f2, ffn2_post_g[l])
    return h
```

```python
import functools
import math

import numpy as np
import jax
import jax.numpy as jnp
from jax import lax
from jax.experimental import pallas as pl
from jax.experimental.pallas import tpu as pltpu

F32 = jnp.float32
BF16 = jnp.bfloat16

D_MODEL = 1024
D_FF = 2816
FFN_RESIDUAL_WEIGHT = 0.5
SSM_D_INNER = 2 * D_MODEL
SSM_HEAD_DIM = 64
SSM_HEADS = SSM_D_INNER // SSM_HEAD_DIM
SSM_GROUPS = 4
SSM_HEADS_PER_GROUP = SSM_HEADS // SSM_GROUPS
SSM_STATE = 128
SSM_CONV = 4
SSM_CHUNK = 128
SSM_BC_DIM = SSM_GROUPS * SSM_STATE
SSM_CONV_DIM = SSM_D_INNER + 2 * SSM_BC_DIM
SSM_GROUP_WIDTH = SSM_D_INNER // SSM_GROUPS
ATTN_Q_HEADS = 16
ATTN_KV_HEADS = 4
ATTN_HEAD_DIM = 64
ATTN_WINDOW = 128
ATTN_BLOCK = 128
ATTN_Q_DIM = ATTN_Q_HEADS * ATTN_HEAD_DIM
ATTN_KV_DIM = ATTN_KV_HEADS * ATTN_HEAD_DIM
REL_BUCKETS = 32
REL_MAX_DISTANCE = 128
RMS_EPS = 1e-6

LANES = 128
SUBLANES = 8
DT_PAD = LANES
CONV_TAIL = SUBLANES

COL_GATES = 0
COL_Z = COL_GATES + 2 * D_MODEL
COL_XBC = COL_Z + SSM_D_INNER
COL_QKV = COL_XBC + SSM_CONV_DIM
COL_DT = COL_QKV + ATTN_Q_DIM + 2 * ATTN_KV_DIM
IN_COLS_PADDED = COL_DT + DT_PAD
QKV_DIM = ATTN_Q_DIM + 2 * ATTN_KV_DIM

FFN_ROWS = 512
MIX_ROWS = 256
VMEM_LIMIT = 56 * 1024 * 1024


def _rms(x, g):
    ms = jnp.mean(x * x, axis=-1, keepdims=True)
    return x * lax.rsqrt(ms + RMS_EPS) * g


def _sigmoid(x):
    return 1.0 / (1.0 + jnp.exp(-x))


def _silu(x):
    return x * _sigmoid(x)


def _softplus(x):
    return jnp.maximum(x, 0.0) + jnp.log1p(jnp.exp(-jnp.abs(x)))


def _dot(a, b):
    return jnp.dot(a, b, preferred_element_type=F32)


def _dot_nt(a, b):
    return lax.dot_general(a, b, (((1,), (1,)), ((), ())), preferred_element_type=F32)


def _split3(x):
    hi = x.astype(BF16)
    r1 = x - hi.astype(F32)
    mid = r1.astype(BF16)
    lo = (r1 - mid.astype(F32)).astype(BF16)
    return hi, mid, lo


def _resident(shape):
    return pl.BlockSpec(shape, lambda *_: (0,) * len(shape), pipeline_mode=pl.Buffered(1))


def _ffn_kernel(x_ref, pre_ref, wg_ref, wu_ref, wd_ref, post_ref, o_ref):
    x = x_ref[...]
    u = _rms(x, pre_ref[...]).astype(BF16)
    g = _dot(u, wg_ref[...])
    up = _dot(u, wu_ref[...])
    hmid = (_silu(g) * up).astype(BF16)
    f = _dot(hmid, wd_ref[...])
    o_ref[...] = x + FFN_RESIDUAL_WEIGHT * _rms(f, post_ref[...])


def _ffn(x2d, pre_g, wg, wu, wd, post_g):
    t = x2d.shape[0]
    row = pl.BlockSpec((FFN_ROWS, D_MODEL), lambda i: (i, 0))
    return pl.pallas_call(
        _ffn_kernel,
        grid=(t // FFN_ROWS,),
        in_specs=[row, _resident((1, D_MODEL)), _resident((D_MODEL, D_FF)), _resident((D_MODEL, D_FF)),
                  _resident((D_FF, D_MODEL)), _resident((1, D_MODEL))],
        out_specs=row,
        out_shape=jax.ShapeDtypeStruct((t, D_MODEL), F32),
        compiler_params=pltpu.CompilerParams(dimension_semantics=("arbitrary",),
                                             vmem_limit_bytes=VMEM_LIMIT),
        name="ffn",
    )(x2d, pre_g, wg, wu, wd, post_g)


def _inproj_kernel(h_ref, g_ref, w_ref, gates_o, z_o, xbc_o, qkv_o, dt_o):
    u = _rms(h_ref[...], g_ref[...]).astype(BF16)
    gates_o[...] = _dot(u, w_ref[:, COL_GATES:COL_Z]).astype(BF16)
    z_o[...] = _dot(u, w_ref[:, COL_Z:COL_XBC]).astype(BF16)
    xbc_o[...] = _dot(u, w_ref[:, COL_XBC:COL_QKV]).astype(BF16)
    qkv_o[...] = _dot(u, w_ref[:, COL_QKV:COL_DT]).astype(BF16)
    dt_o[...] = _dot(u, w_ref[:, COL_DT:IN_COLS_PADDED])


def _inproj(h2d, g, w):
    t = h2d.shape[0]

    def row(width):
        return pl.BlockSpec((FFN_ROWS, width), lambda i: (i, 0))

    widths = (2 * D_MODEL, SSM_D_INNER, SSM_CONV_DIM, QKV_DIM, DT_PAD)
    dtypes = (BF16, BF16, BF16, BF16, F32)
    return pl.pallas_call(
        _inproj_kernel,
        grid=(t // FFN_ROWS,),
        in_specs=[row(D_MODEL), _resident((1, D_MODEL)), _resident((D_MODEL, IN_COLS_PADDED))],
        out_specs=[row(w_) for w_ in widths],
        out_shape=[jax.ShapeDtypeStruct((t, w_), d_) for w_, d_ in zip(widths, dtypes)],
        compiler_params=pltpu.CompilerParams(dimension_semantics=("arbitrary",),
                                             vmem_limit_bytes=VMEM_LIMIT),
        name="inproj",
    )(h2d, g, w)


def _mixer_kernel(sinks_ref, table_ref,
                  h_ref, gates_ref, z_ref, xbc_ref, qkv_ref, dt_ref,
                  convw_ref, convb_ref, dtb_ref, alog_ref, dexp_ref, ng_ref, bucket_ref,
                  wssm_ref, wattn_ref, wout_ref, postg_ref,
                  o_ref,
                  xpad, xact, kvpad, yscr, ynorm, attn, state, bias):
    rows_per_step = h_ref.shape[0]
    n_chunks = rows_per_step // SSM_CHUNK
    b = pl.program_id(0)
    t = pl.program_id(1)
    q = SSM_CHUNK

    ii = lax.broadcasted_iota(jnp.int32, (q, q), 0)
    jj = lax.broadcasted_iota(jnp.int32, (q, q), 1)
    causal = ii >= jj
    tril = jnp.where(causal, 1.0, 0.0).astype(BF16)

    @pl.when((b == 0) & (t == 0))
    def _init_bias():
        bi = lax.broadcasted_iota(jnp.int32, (ATTN_BLOCK, 2 * ATTN_BLOCK), 0)
        bj = lax.broadcasted_iota(jnp.int32, (ATTN_BLOCK, 2 * ATTN_BLOCK), 1)
        dist = bi + ATTN_BLOCK - bj
        in_window = (dist >= 0) & (dist < ATTN_WINDOW)
        bucket = bucket_ref[...]
        for hq in range(ATTN_Q_HEADS):
            acc = jnp.zeros((ATTN_BLOCK, 2 * ATTN_BLOCK), F32)
            for bk in range(REL_BUCKETS):
                acc = jnp.where(bucket == bk, table_ref[bk, hq], acc)
            bias[hq] = jnp.where(in_window, acc, -jnp.inf)

    @pl.when(t == 0)
    def _reset_sequence_state():
        state[...] = jnp.zeros_like(state)
        xpad[0:CONV_TAIL, :] = jnp.zeros((CONV_TAIL, SSM_CONV_DIM), F32)
        kvpad[0:ATTN_BLOCK, :] = jnp.zeros((ATTN_BLOCK, 2 * ATTN_KV_DIM), BF16)

    xpad[CONV_TAIL:CONV_TAIL + rows_per_step, :] = xbc_ref[...].astype(F32)
    conv = convb_ref[...]
    for k in range(SSM_CONV):
        start = CONV_TAIL - (SSM_CONV - 1) + k
        conv = conv + convw_ref[k:k + 1, :] * xpad[start:start + rows_per_step, :]
    xact[...] = _silu(conv).astype(BF16)
    xpad[0:CONV_TAIL, :] = xpad[rows_per_step:rows_per_step + CONV_TAIL, :]

    kvpad[ATTN_BLOCK:ATTN_BLOCK + rows_per_step, :] = qkv_ref[:, ATTN_Q_DIM:QKV_DIM]

    a_neg = -jnp.exp(alog_ref[...])
    key_idx = lax.broadcasted_iota(jnp.int32, (ATTN_BLOCK, 2 * ATTN_BLOCK), 1)

    def chunk_body(c, carry):
        r0 = pl.multiple_of(c * q, q)
        rows = pl.ds(r0, q)

        dt = _softplus(dt_ref[rows, :] + dtb_ref[...])
        d_a = dt * a_neg
        hi, mid, lo = _split3(d_a)
        acs = _dot(tril, hi) + _dot(tril, mid) + _dot(tril, lo)
        acs_t = acs.T
        dt_t = dt.T

        for g in range(SSM_GROUPS):
            b_g = xact[rows, SSM_D_INNER + g * SSM_STATE:SSM_D_INNER + (g + 1) * SSM_STATE]
            c_g = xact[rows, SSM_D_INNER + SSM_BC_DIM + g * SSM_STATE:
                       SSM_D_INNER + SSM_BC_DIM + (g + 1) * SSM_STATE]
            scores = _dot_nt(c_g, b_g)
            b_gt = b_g.astype(F32).T
            h_prev = state[g]
            y_off = _dot(c_g, h_prev.astype(BF16))
            for r in range(SSM_HEADS_PER_GROUP):
                h = g * SSM_HEADS_PER_GROUP + r
                lanes = slice(h * SSM_HEAD_DIM, (h + 1) * SSM_HEAD_DIM)
                sub = slice(r * SSM_HEAD_DIM, (r + 1) * SSM_HEAD_DIM)
                col = jnp.broadcast_to(acs[:, h:h + 1], (q, q))
                row = acs_t[h:h + 1, :]
                dt_row = dt_t[h:h + 1, :]
                decay = jnp.exp(jnp.where(causal, col - row, -jnp.inf))
                l_mat = (decay * scores * dt_row).astype(BF16)
                xs_h = xact[rows, lanes]
                y_h = _dot(l_mat, xs_h) + y_off[:, sub] * jnp.exp(col[:, :SSM_HEAD_DIM])
                yscr[:, lanes] = y_h
                last = acs_t[h:h + 1, q - 1:q]
                w_row = dt_row * jnp.exp(last - row)
                s_new = _dot((b_gt * w_row).astype(BF16), xs_h)
                state[g, :, sub] = h_prev[:, sub] * jnp.exp(last) + s_new

        xs = xact[rows, 0:SSM_D_INNER].astype(F32)
        y = yscr[...] + dexp_ref[...] * xs
        yg = y * _silu(z_ref[rows, :].astype(F32))
        for g in range(SSM_GROUPS):
            gl = slice(g * SSM_GROUP_WIDTH, (g + 1) * SSM_GROUP_WIDTH)
            v = yg[:, gl]
            ms = jnp.mean(v * v, axis=-1, keepdims=True)
            ynorm[rows, gl] = (v * lax.rsqrt(ms + RMS_EPS) * ng_ref[:, gl]).astype(BF16)

        first_block = (t == 0) & (c == 0)
        key_exists = key_idx >= jnp.where(first_block, ATTN_BLOCK, 0)
        band = pl.ds(r0, 2 * ATTN_BLOCK)
        for hq in range(ATTN_Q_HEADS):
            kv = hq // (ATTN_Q_HEADS // ATTN_KV_HEADS)
            q_h = qkv_ref[rows, hq * ATTN_HEAD_DIM:(hq + 1) * ATTN_HEAD_DIM]
            k_h = kvpad[band, kv * ATTN_HEAD_DIM:(kv + 1) * ATTN_HEAD_DIM]
            v_h = kvpad[band, ATTN_KV_DIM + kv * ATTN_HEAD_DIM:ATTN_KV_DIM + (kv + 1) * ATTN_HEAD_DIM]
            logits = _dot_nt(q_h, k_h) * (ATTN_HEAD_DIM ** -0.5) + bias[hq]
            logits = jnp.where(key_exists, logits, -jnp.inf)
            sink = sinks_ref[hq]
            m = jnp.maximum(jnp.max(logits, axis=-1, keepdims=True), sink)
            p = jnp.exp(logits - m)
            den = jnp.sum(p, axis=-1, keepdims=True) + jnp.exp(sink - m)
            o_h = _dot(p.astype(BF16), v_h) / den
            attn[rows, hq * ATTN_HEAD_DIM:(hq + 1) * ATTN_HEAD_DIM] = o_h.astype(BF16)
        return carry

    lax.fori_loop(0, n_chunks, chunk_body, 0)

    kvpad[0:ATTN_BLOCK, :] = kvpad[rows_per_step:rows_per_step + ATTN_BLOCK, :]

    y_ssm = _dot(ynorm[...], wssm_ref[...])
    y_attn = _dot(attn[...], wattn_ref[...])
    gates = gates_ref[...].astype(F32)
    merged = _sigmoid(gates[:, :D_MODEL]) * y_ssm + _sigmoid(gates[:, D_MODEL:]) * y_attn
    mix = _dot(merged.astype(BF16), wout_ref[...])
    o_ref[...] = h_ref[...] + _rms(mix, postg_ref[...])


def _t5_bucket_matrix():
    qi = np.arange(ATTN_BLOCK)[:, None]
    kj = np.arange(2 * ATTN_BLOCK)[None, :]
    dist = np.maximum(qi + ATTN_BLOCK - kj, 0).astype(np.int32)
    max_exact = REL_BUCKETS // 2
    d = np.maximum(dist, 1).astype(np.float32)
    large = max_exact + (np.log(d / np.float32(max_exact)) / np.float32(math.log(REL_MAX_DISTANCE / max_exact))
                         * np.float32(REL_BUCKETS - max_exact)).astype(np.int32)
    large = np.minimum(large, REL_BUCKETS - 1)
    return np.where(dist < max_exact, dist, large).astype(np.int32)


def _mixer(h2d, gates, z, xbc, qkv, dt, batch, seq, conv_w, conv_b, dt_bias, a_log, d_exp, norm_g,
           w_ssm, w_attn, w_out, post_g, sinks, table):
    nt = seq // MIX_ROWS

    def row(width):
        return pl.BlockSpec((MIX_ROWS, width), lambda b, t: (b * nt + t, 0))

    smem = pl.BlockSpec(memory_space=pltpu.SMEM)
    bucket = jnp.asarray(_t5_bucket_matrix())
    in_specs = [
        smem, smem,
        row(D_MODEL), row(2 * D_MODEL), row(SSM_D_INNER), row(SSM_CONV_DIM), row(QKV_DIM), row(DT_PAD),
        _resident((SSM_CONV, SSM_CONV_DIM)), _resident((1, SSM_CONV_DIM)), _resident((1, DT_PAD)),
        _resident((1, DT_PAD)), _resident((1, SSM_D_INNER)), _resident((1, SSM_D_INNER)),
        _resident((ATTN_BLOCK, 2 * ATTN_BLOCK)),
        _resident((SSM_D_INNER, D_MODEL)), _resident((ATTN_Q_DIM, D_MODEL)), _resident((D_MODEL, D_MODEL)),
        _resident((1, D_MODEL)),
    ]
    scratch = [
        pltpu.VMEM((CONV_TAIL + MIX_ROWS, SSM_CONV_DIM), F32),
        pltpu.VMEM((MIX_ROWS, SSM_CONV_DIM), BF16),
        pltpu.VMEM((ATTN_BLOCK + MIX_ROWS, 2 * ATTN_KV_DIM), BF16),
        pltpu.VMEM((SSM_CHUNK, SSM_D_INNER), F32),
        pltpu.VMEM((MIX_ROWS, SSM_D_INNER), BF16),
        pltpu.VMEM((MIX_ROWS, ATTN_Q_DIM), BF16),
        pltpu.VMEM((SSM_GROUPS, SSM_STATE, SSM_HEADS_PER_GROUP * SSM_HEAD_DIM), F32),
        pltpu.VMEM((ATTN_Q_HEADS, ATTN_BLOCK, 2 * ATTN_BLOCK), F32),
    ]
    return pl.pallas_call(
        _mixer_kernel,
        grid=(batch, nt),
        in_specs=in_specs,
        out_specs=row(D_MODEL),
        out_shape=jax.ShapeDtypeStruct((batch * seq, D_MODEL), F32),
        scratch_shapes=scratch,
        compiler_params=pltpu.CompilerParams(dimension_semantics=("arbitrary", "arbitrary"),
                                             vmem_limit_bytes=VMEM_LIMIT),
        name="mixer",
    )(sinks, table, h2d, gates, z, xbc, qkv, dt, conv_w, conv_b, dt_bias, a_log, d_exp, norm_g, bucket,
      w_ssm, w_attn, w_out, post_g)


def _reorder_w_in(w_in):
    o_gates = 0
    o_z = o_gates + 2 * D_MODEL
    o_xbc = o_z + SSM_D_INNER
    o_dt = o_xbc + SSM_CONV_DIM
    o_qkv = o_dt + SSM_HEADS
    dt_cols = jnp.pad(w_in[:, o_dt:o_qkv], ((0, 0), (0, DT_PAD - SSM_HEADS)))
    return jnp.concatenate([w_in[:, :o_dt], w_in[:, o_qkv:], dt_cols], axis=1)


def _pad_heads(v):
    return jnp.pad(v.astype(F32), (0, DT_PAD - SSM_HEADS))[None, :]


def kernel(x, ffn1_pre_g, ffn1_w_gate, ffn1_w_up, ffn1_w_down, ffn1_post_g, mix_pre_g, w_in, conv_w, conv_b, dt_bias, a_log, d_skip, ssm_norm_g, w_ssm_proj, attn_sinks, rel_bias_table, w_attn_proj, w_out, mix_post_g, ffn2_pre_g, ffn2_w_gate, ffn2_w_up, ffn2_w_down, ffn2_post_g):
    batch, seq, _ = x.shape
    depth = w_in.shape[0]
    h = x.reshape(batch * seq, D_MODEL)
    table = rel_bias_table.astype(F32)
    for l in range(depth):
        h = _ffn(h, ffn1_pre_g[l][None, :], ffn1_w_gate[l].astype(BF16), ffn1_w_up[l].astype(BF16),
                 ffn1_w_down[l].astype(BF16), ffn1_post_g[l][None, :])
        gates, z, xbc, qkv, dt = _inproj(h, mix_pre_g[l][None, :], _reorder_w_in(w_in[l]).astype(BF16))
        h = _mixer(h, gates, z, xbc, qkv, dt, batch, seq,
                   conv_w[l], conv_b[l][None, :], _pad_heads(dt_bias[l]), _pad_heads(a_log[l]),
                   jnp.repeat(d_skip[l].astype(F32), SSM_HEAD_DIM)[None, :], ssm_norm_g[l][None, :],
                   w_ssm_proj[l].astype(BF16), w_attn_proj[l].astype(BF16), w_out[l].astype(BF16),
                   mix_post_g[l][None, :], attn_sinks[l].astype(F32), table)
        h = _ffn(h, ffn2_pre_g[l][None, :], ffn2_w_gate[l].astype(BF16), ffn2_w_up[l].astype(BF16),
                 ffn2_w_down[l].astype(BF16), ffn2_post_g[l][None, :])
    return h.reshape(batch, seq, D_MODEL)
```

```python
import math

import numpy as np
import jax
import jax.numpy as jnp
from jax import lax
from jax.experimental import pallas as pl
from jax.experimental.pallas import tpu as pltpu

F32 = jnp.float32
BF16 = jnp.bfloat16

D_MODEL = 1024
D_FF = 2816
FFN_RESIDUAL_WEIGHT = 0.5
SSM_D_INNER = 2 * D_MODEL
SSM_HEAD_DIM = 64
SSM_HEADS = SSM_D_INNER // SSM_HEAD_DIM
SSM_GROUPS = 4
SSM_STATE = 128
SSM_CONV = 4
SSM_CHUNK = 128
SSM_BC_DIM = SSM_GROUPS * SSM_STATE
SSM_CONV_DIM = SSM_D_INNER + 2 * SSM_BC_DIM
SSM_GROUP_WIDTH = SSM_D_INNER // SSM_GROUPS
ATTN_Q_HEADS = 16
ATTN_KV_HEADS = 4
ATTN_HEAD_DIM = 64
ATTN_WINDOW = 128
ATTN_BLOCK = 128
ATTN_Q_DIM = ATTN_Q_HEADS * ATTN_HEAD_DIM
ATTN_KV_DIM = ATTN_KV_HEADS * ATTN_HEAD_DIM
REL_BUCKETS = 32
REL_MAX_DISTANCE = 128
RMS_EPS = 1e-6
LOG2E = math.log2(math.e)

LANES = 128
SUBLANES = 8
MXU_WIDTH = 256
DT_PAD = LANES
CONV_TAIL = SUBLANES

SLAB_HEADS = MXU_WIDTH // SSM_HEAD_DIM
SSM_SLABS = SSM_D_INNER // MXU_WIDTH
SLABS_PER_GROUP = SSM_SLABS // SSM_GROUPS
ATTN_PAIRS = ATTN_Q_DIM // LANES
PAIRS_PER_KV = ATTN_PAIRS // ATTN_KV_HEADS
KV_DUP_DIM = ATTN_KV_HEADS * LANES

COL_GATES = 0
COL_Z = COL_GATES + 2 * D_MODEL
COL_XBC = COL_Z + SSM_D_INNER
COL_QKV = COL_XBC + SSM_CONV_DIM
QKV_DIM = ATTN_Q_DIM + 2 * KV_DUP_DIM
COL_DT = COL_QKV + QKV_DIM
IN_COLS_PADDED = COL_DT + DT_PAD

FFN_ROWS = 512
MIX_ROWS = 256
VMEM_LIMIT = 56 * 1024 * 1024


def _rms(x, g):
    ms = jnp.mean(x * x, axis=-1, keepdims=True)
    return x * lax.rsqrt(ms + RMS_EPS) * g


def _sigmoid(x):
    return 1.0 / (1.0 + jnp.exp(-x))


def _silu(x):
    return x * _sigmoid(x)


def _softplus(x):
    return jnp.maximum(x, 0.0) + jnp.log1p(jnp.exp(-jnp.abs(x)))


def _dot(a, b):
    return jnp.dot(a, b, preferred_element_type=F32)


def _dot_nt(a, b):
    return lax.dot_general(a, b, (((1,), (1,)), ((), ())), preferred_element_type=F32)


def _split3(x):
    hi = x.astype(BF16)
    r1 = x - hi.astype(F32)
    mid = r1.astype(BF16)
    lo = (r1 - mid.astype(F32)).astype(BF16)
    return hi, mid, lo


def _resident(shape):
    return pl.BlockSpec(shape, lambda *_: (0,) * len(shape), pipeline_mode=pl.Buffered(1))


def _ffn_kernel(x_ref, pre_ref, wg_ref, wu_ref, wd_ref, post_ref, o_ref):
    x = x_ref[...]
    u = _rms(x, pre_ref[...]).astype(BF16)
    g = _dot(u, wg_ref[...])
    up = _dot(u, wu_ref[...])
    hmid = (_silu(g) * up).astype(BF16)
    f = _dot(hmid, wd_ref[...])
    o_ref[...] = x + FFN_RESIDUAL_WEIGHT * _rms(f, post_ref[...])


def _ffn(x2d, pre_g, wg, wu, wd, post_g):
    t = x2d.shape[0]
    row = pl.BlockSpec((FFN_ROWS, D_MODEL), lambda i: (i, 0))
    return pl.pallas_call(
        _ffn_kernel,
        grid=(t // FFN_ROWS,),
        in_specs=[row, _resident((1, D_MODEL)), _resident((D_MODEL, D_FF)), _resident((D_MODEL, D_FF)),
                  _resident((D_FF, D_MODEL)), _resident((1, D_MODEL))],
        out_specs=row,
        out_shape=jax.ShapeDtypeStruct((t, D_MODEL), F32),
        compiler_params=pltpu.CompilerParams(dimension_semantics=("arbitrary",),
                                             vmem_limit_bytes=VMEM_LIMIT),
        name="ffn",
    )(x2d, pre_g, wg, wu, wd, post_g)


def _inproj_kernel(h_ref, g_ref, w_ref, gates_o, z_o, xbc_o, qkv_o, dt_o):
    u = _rms(h_ref[...], g_ref[...]).astype(BF16)
    gates_o[...] = _dot(u, w_ref[:, COL_GATES:COL_Z]).astype(BF16)
    z_o[...] = _dot(u, w_ref[:, COL_Z:COL_XBC]).astype(BF16)
    xbc_o[...] = _dot(u, w_ref[:, COL_XBC:COL_QKV]).astype(BF16)
    qkv_o[...] = _dot(u, w_ref[:, COL_QKV:COL_DT]).astype(BF16)
    dt_o[...] = _dot(u, w_ref[:, COL_DT:IN_COLS_PADDED])


def _inproj(h2d, g, w):
    t = h2d.shape[0]

    def row(width):
        return pl.BlockSpec((FFN_ROWS, width), lambda i: (i, 0))

    widths = (2 * D_MODEL, SSM_D_INNER, SSM_CONV_DIM, QKV_DIM, DT_PAD)
    dtypes = (BF16, BF16, BF16, BF16, F32)
    return pl.pallas_call(
        _inproj_kernel,
        grid=(t // FFN_ROWS,),
        in_specs=[row(D_MODEL), _resident((1, D_MODEL)), _resident((D_MODEL, IN_COLS_PADDED))],
        out_specs=[row(w_) for w_ in widths],
        out_shape=[jax.ShapeDtypeStruct((t, w_), d_) for w_, d_ in zip(widths, dtypes)],
        compiler_params=pltpu.CompilerParams(dimension_semantics=("arbitrary",),
                                             vmem_limit_bytes=VMEM_LIMIT),
        name="inproj",
    )(h2d, g, w)


def _mixer_kernel(sinks_ref, table_ref,
                  h_ref, gates_ref, z_ref, xbc_ref, qkv_ref, dt_ref,
                  convw_ref, convb_ref, dtb_ref, alog_ref, dexp_ref, ng_ref, bucket_ref,
                  wssm_ref, wattn_ref, wout_ref, postg_ref,
                  o_ref,
                  xpad, xact, kvpad, yscr, ynorm, attn, state, bias):
    rows_per_step = h_ref.shape[0]
    n_chunks = rows_per_step // SSM_CHUNK
    b = pl.program_id(0)
    t = pl.program_id(1)
    q = SSM_CHUNK

    ii = lax.broadcasted_iota(jnp.int32, (q, q), 0)
    jj = lax.broadcasted_iota(jnp.int32, (q, q), 1)
    causal = ii >= jj
    tril = jnp.where(causal, 1.0, 0.0).astype(BF16)
    low_half = jj < SSM_HEAD_DIM
    low_half_row = low_half[0:1, :]
    slab_lane_head = lax.broadcasted_iota(jnp.int32, (q, MXU_WIDTH), 1) // SSM_HEAD_DIM
    band_low_half = lax.broadcasted_iota(jnp.int32, (2 * ATTN_BLOCK, LANES), 1) < ATTN_HEAD_DIM

    @pl.when((b == 0) & (t == 0))
    def _init_bias():
        bi = lax.broadcasted_iota(jnp.int32, (ATTN_BLOCK, 2 * ATTN_BLOCK), 0)
        bj = lax.broadcasted_iota(jnp.int32, (ATTN_BLOCK, 2 * ATTN_BLOCK), 1)
        dist = bi + ATTN_BLOCK - bj
        in_window = (dist >= 0) & (dist < ATTN_WINDOW)
        bucket = bucket_ref[...]
        for hq in range(ATTN_Q_HEADS):
            acc = jnp.zeros((ATTN_BLOCK, 2 * ATTN_BLOCK), F32)
            for bk in range(REL_BUCKETS):
                acc = jnp.where(bucket == bk, table_ref[bk, hq], acc)
            masked = jnp.where(in_window, acc, -jnp.inf)
            bias[0, hq] = masked
            bias[1, hq] = jnp.where(bj >= ATTN_BLOCK, masked, -jnp.inf)

    @pl.when(t == 0)
    def _reset_sequence_state():
        state[...] = jnp.zeros_like(state)
        xpad[0:CONV_TAIL, :] = jnp.zeros((CONV_TAIL, SSM_CONV_DIM), F32)
        kvpad[0:ATTN_BLOCK, :] = jnp.zeros((ATTN_BLOCK, 2 * KV_DUP_DIM), BF16)

    xpad[CONV_TAIL:CONV_TAIL + rows_per_step, :] = xbc_ref[...].astype(F32)
    conv = convb_ref[...]
    for k in range(SSM_CONV):
        start = CONV_TAIL - (SSM_CONV - 1) + k
        conv = conv + convw_ref[k:k + 1, :] * xpad[start:start + rows_per_step, :]
    xact[...] = _silu(conv).astype(BF16)
    xpad[0:CONV_TAIL, :] = xpad[rows_per_step:rows_per_step + CONV_TAIL, :]

    kvpad[ATTN_BLOCK:ATTN_BLOCK + rows_per_step, :] = qkv_ref[:, ATTN_Q_DIM:QKV_DIM]

    a_neg = -jnp.exp(alog_ref[...])

    def chunk_body(c, carry):
        r0 = pl.multiple_of(c * q, q)
        rows = pl.ds(r0, q)

        dt = _softplus(dt_ref[rows, :] + dtb_ref[...])
        hi, mid, lo = _split3(dt * a_neg)
        acs2 = (_dot(tril, hi) + _dot(tril, mid) + _dot(tril, lo)) * LOG2E
        row2_t = (acs2 - jnp.log2(dt)).T

        for m in range(SSM_SLABS):
            g = m // SLABS_PER_GROUP
            if m % SLABS_PER_GROUP == 0:
                b_g = xact[rows, SSM_D_INNER + g * SSM_STATE:SSM_D_INNER + (g + 1) * SSM_STATE]
                c_g = xact[rows, SSM_D_INNER + SSM_BC_DIM + g * SSM_STATE:
                           SSM_D_INNER + SSM_BC_DIM + (g + 1) * SSM_STATE]
                scores = _dot_nt(c_g, b_g)
                b_gt = b_g.astype(F32).T
            tops, bots, ebs, cds = [], [], [], []
            for r in range(SLAB_HEADS):
                h = m * SLAB_HEADS + r
                colb = jnp.broadcast_to(acs2[:, h:h + 1], (q, q))
                row2 = row2_t[h:h + 1, :]
                lastb = colb[q - 1:q, :]
                l_mat = jnp.exp2(jnp.where(causal, colb - row2, -jnp.inf)) * scores
                tops.append(l_mat.astype(BF16))
                bots.append((b_gt * jnp.exp2(lastb - row2)).astype(BF16))
                ebs.append(jnp.exp2(colb))
                cds.append(jnp.exp2(lastb))
            lhs = jnp.concatenate([jnp.concatenate(tops, axis=1), jnp.concatenate(bots, axis=1)], axis=0)
            xs_slab = xact[rows, m * MXU_WIDTH:(m + 1) * MXU_WIDTH]
            rhs = jnp.concatenate(
                [jnp.where(slab_lane_head == r, xs_slab, jnp.zeros_like(xs_slab)) for r in range(SLAB_HEADS)],
                axis=0)
            yy = _dot(lhs, rhs)
            h_prev = state[m]
            y_off = _dot(c_g, h_prev.astype(BF16))
            e_slab = jnp.concatenate([jnp.where(low_half, ebs[0], ebs[1]),
                                      jnp.where(low_half, ebs[2], ebs[3])], axis=1)
            cd_slab = jnp.concatenate([jnp.where(low_half_row, cds[0], cds[1]),
                                       jnp.where(low_half_row, cds[2], cds[3])], axis=1)
            yscr[:, m * MXU_WIDTH:(m + 1) * MXU_WIDTH] = yy[0:q] + y_off * e_slab
            state[m] = h_prev * cd_slab + yy[q:2 * q]

        xs = xact[rows, 0:SSM_D_INNER].astype(F32)
        y = yscr[...] + dexp_ref[...] * xs
        yg = y * _silu(z_ref[rows, :].astype(F32))
        for g in range(SSM_GROUPS):
            gl = slice(g * SSM_GROUP_WIDTH, (g + 1) * SSM_GROUP_WIDTH)
            v = yg[:, gl]
            ms = jnp.mean(v * v, axis=-1, keepdims=True)
            ynorm[rows, gl] = (v * lax.rsqrt(ms + RMS_EPS) * ng_ref[:, gl]).astype(BF16)

        first = jnp.where((t == 0) & (c == 0), 1, 0)
        band = pl.ds(r0, 2 * ATTN_BLOCK)
        for kv in range(ATTN_KV_HEADS):
            kd = kvpad[band, kv * LANES:(kv + 1) * LANES]
            vd = kvpad[band, KV_DUP_DIM + kv * LANES:KV_DUP_DIM + (kv + 1) * LANES]
            zero = jnp.zeros_like(kd)
            k_cat = jnp.concatenate([jnp.where(band_low_half, kd, zero), jnp.where(band_low_half, zero, kd)], axis=0)
            v_cat = jnp.concatenate([jnp.where(band_low_half, vd, zero), jnp.where(band_low_half, zero, vd)], axis=0)
            for pp in range(PAIRS_PER_KV):
                pair = kv * PAIRS_PER_KV + pp
                q_pair = qkv_ref[rows, pair * LANES:(pair + 1) * LANES]
                logits = _dot_nt(q_pair, k_cat)
                ps, invs = [], []
                for s in range(2):
                    hq = 2 * pair + s
                    l2 = logits[:, s * 2 * ATTN_BLOCK:(s + 1) * 2 * ATTN_BLOCK] + bias[first, hq]
                    sink = sinks_ref[hq]
                    mx = jnp.maximum(jnp.max(l2, axis=-1, keepdims=True), sink)
                    p = jnp.exp2(l2 - mx)
                    den = jnp.sum(p, axis=-1, keepdims=True) + jnp.exp2(sink - mx)
                    ps.append(p.astype(BF16))
                    invs.append(1.0 / den)
                o = _dot(jnp.concatenate(ps, axis=1), v_cat)
                o = o * jnp.where(low_half, invs[0], invs[1])
                attn[rows, pair * LANES:(pair + 1) * LANES] = o.astype(BF16)
        return carry

    lax.fori_loop(0, n_chunks, chunk_body, 0)

    kvpad[0:ATTN_BLOCK, :] = kvpad[rows_per_step:rows_per_step + ATTN_BLOCK, :]

    y_ssm = _dot(ynorm[...], wssm_ref[...])
    y_attn = _dot(attn[...], wattn_ref[...])
    gates = gates_ref[...].astype(F32)
    merged = _sigmoid(gates[:, :D_MODEL]) * y_ssm + _sigmoid(gates[:, D_MODEL:]) * y_attn
    mix = _dot(merged.astype(BF16), wout_ref[...])
    o_ref[...] = h_ref[...] + _rms(mix, postg_ref[...])


def _t5_bucket_matrix():
    qi = np.arange(ATTN_BLOCK)[:, None]
    kj = np.arange(2 * ATTN_BLOCK)[None, :]
    dist = np.maximum(qi + ATTN_BLOCK - kj, 0).astype(np.int32)
    max_exact = REL_BUCKETS // 2
    d = np.maximum(dist, 1).astype(np.float32)
    large = max_exact + (np.log(d / np.float32(max_exact)) / np.float32(math.log(REL_MAX_DISTANCE / max_exact))
                         * np.float32(REL_BUCKETS - max_exact)).astype(np.int32)
    large = np.minimum(large, REL_BUCKETS - 1)
    return np.where(dist < max_exact, dist, large).astype(np.int32)


def _mixer(h2d, gates, z, xbc, qkv, dt, batch, seq, conv_w, conv_b, dt_bias, a_log, d_exp, norm_g,
           w_ssm, w_attn, w_out, post_g, sinks, table):
    nt = seq // MIX_ROWS

    def row(width):
        return pl.BlockSpec((MIX_ROWS, width), lambda b, t: (b * nt + t, 0))

    smem = pl.BlockSpec(memory_space=pltpu.SMEM)
    bucket = jnp.asarray(_t5_bucket_matrix())
    in_specs = [
        smem, smem,
        row(D_MODEL), row(2 * D_MODEL), row(SSM_D_INNER), row(SSM_CONV_DIM), row(QKV_DIM), row(DT_PAD),
        _resident((SSM_CONV, SSM_CONV_DIM)), _resident((1, SSM_CONV_DIM)), _resident((1, DT_PAD)),
        _resident((1, DT_PAD)), _resident((1, SSM_D_INNER)), _resident((1, SSM_D_INNER)),
        _resident((ATTN_BLOCK, 2 * ATTN_BLOCK)),
        _resident((SSM_D_INNER, D_MODEL)), _resident((ATTN_Q_DIM, D_MODEL)), _resident((D_MODEL, D_MODEL)),
        _resident((1, D_MODEL)),
    ]
    scratch = [
        pltpu.VMEM((CONV_TAIL + MIX_ROWS, SSM_CONV_DIM), F32),
        pltpu.VMEM((MIX_ROWS, SSM_CONV_DIM), BF16),
        pltpu.VMEM((ATTN_BLOCK + MIX_ROWS, 2 * KV_DUP_DIM), BF16),
        pltpu.VMEM((SSM_CHUNK, SSM_D_INNER), F32),
        pltpu.VMEM((MIX_ROWS, SSM_D_INNER), BF16),
        pltpu.VMEM((MIX_ROWS, ATTN_Q_DIM), BF16),
        pltpu.VMEM((SSM_SLABS, SSM_STATE, MXU_WIDTH), F32),
        pltpu.VMEM((2, ATTN_Q_HEADS, ATTN_BLOCK, 2 * ATTN_BLOCK), F32),
    ]
    return pl.pallas_call(
        _mixer_kernel,
        grid=(batch, nt),
        in_specs=in_specs,
        out_specs=row(D_MODEL),
        out_shape=jax.ShapeDtypeStruct((batch * seq, D_MODEL), F32),
        scratch_shapes=scratch,
        compiler_params=pltpu.CompilerParams(dimension_semantics=("arbitrary", "arbitrary"),
                                             vmem_limit_bytes=VMEM_LIMIT),
        name="mixer",
    )(sinks, table, h2d, gates, z, xbc, qkv, dt, conv_w, conv_b, dt_bias, a_log, d_exp, norm_g, bucket,
      w_ssm, w_attn, w_out, post_g)


def _reorder_w_in(w_in):
    o_z = 2 * D_MODEL
    o_dt = o_z + SSM_D_INNER + SSM_CONV_DIM
    o_q = o_dt + SSM_HEADS
    o_k = o_q + ATTN_Q_DIM
    o_v = o_k + ATTN_KV_DIM

    def dup_heads(w):
        w = w.reshape(D_MODEL, ATTN_KV_HEADS, 1, ATTN_HEAD_DIM)
        return jnp.broadcast_to(w, (D_MODEL, ATTN_KV_HEADS, 2, ATTN_HEAD_DIM)).reshape(D_MODEL, KV_DUP_DIM)

    w_q = w_in[:, o_q:o_k] * (ATTN_HEAD_DIM ** -0.5 * LOG2E)
    dt_cols = jnp.pad(w_in[:, o_dt:o_q], ((0, 0), (0, DT_PAD - SSM_HEADS)))
    return jnp.concatenate([w_in[:, :o_dt], w_q, dup_heads(w_in[:, o_k:o_v]), dup_heads(w_in[:, o_v:]), dt_cols],
                           axis=1)


def _pad_heads(v):
    return jnp.pad(v.astype(F32), (0, DT_PAD - SSM_HEADS))[None, :]


def kernel(x, ffn1_pre_g, ffn1_w_gate, ffn1_w_up, ffn1_w_down, ffn1_post_g, mix_pre_g, w_in, conv_w, conv_b, dt_bias, a_log, d_skip, ssm_norm_g, w_ssm_proj, attn_sinks, rel_bias_table, w_attn_proj, w_out, mix_post_g, ffn2_pre_g, ffn2_w_gate, ffn2_w_up, ffn2_w_down, ffn2_post_g):
    batch, seq, _ = x.shape
    depth = w_in.shape[0]
    h = x.reshape(batch * seq, D_MODEL)
    table = rel_bias_table.astype(F32) * LOG2E
    for l in range(depth):
        h = _ffn(h, ffn1_pre_g[l][None, :], ffn1_w_gate[l].astype(BF16), ffn1_w_up[l].astype(BF16),
                 ffn1_w_down[l].astype(BF16), ffn1_post_g[l][None, :])
        gates, z, xbc, qkv, dt = _inproj(h, mix_pre_g[l][None, :], _reorder_w_in(w_in[l]).astype(BF16))
        h = _mixer(h, gates, z, xbc, qkv, dt, batch, seq,
                   conv_w[l], conv_b[l][None, :], _pad_heads(dt_bias[l]), _pad_heads(a_log[l]),
                   jnp.repeat(d_skip[l].astype(F32), SSM_HEAD_DIM)[None, :], ssm_norm_g[l][None, :],
                   w_ssm_proj[l].astype(BF16), w_attn_proj[l].astype(BF16), w_out[l].astype(BF16),
                   mix_post_g[l][None, :], attn_sinks[l].astype(F32) * LOG2E, table)
        h = _ffn(h, ffn2_pre_g[l][None, :], ffn2_w_gate[l].astype(BF16), ffn2_w_up[l].astype(BF16),
                 ffn2_w_down[l].astype(BF16), ffn2_post_g[l][None, :])
    return h.reshape(batch, seq, D_MODEL)
```

```python
import functools
import math

import numpy as np
import jax
import jax.numpy as jnp
from jax import lax
from jax.experimental import pallas as pl
from jax.experimental.pallas import tpu as pltpu

F32 = jnp.float32
BF16 = jnp.bfloat16

D_MODEL = 1024
D_FF = 2816
FFN_RESIDUAL_WEIGHT = 0.5
SSM_D_INNER = 2 * D_MODEL
SSM_HEAD_DIM = 64
SSM_HEADS = SSM_D_INNER // SSM_HEAD_DIM
SSM_GROUPS = 4
SSM_STATE = 128
SSM_CONV = 4
SSM_CHUNK = 128
SSM_BC_DIM = SSM_GROUPS * SSM_STATE
SSM_CONV_DIM = SSM_D_INNER + 2 * SSM_BC_DIM
SSM_GROUP_WIDTH = SSM_D_INNER // SSM_GROUPS
ATTN_Q_HEADS = 16
ATTN_KV_HEADS = 4
ATTN_HEAD_DIM = 64
ATTN_WINDOW = 128
ATTN_BLOCK = 128
ATTN_Q_DIM = ATTN_Q_HEADS * ATTN_HEAD_DIM
ATTN_KV_DIM = ATTN_KV_HEADS * ATTN_HEAD_DIM
REL_BUCKETS = 32
REL_MAX_DISTANCE = 128
RMS_EPS = 1e-6
LOG2E = math.log2(math.e)

LANES = 128
SUBLANES = 8
MXU_WIDTH = 256
DT_PAD = LANES
CONV_TAIL = SUBLANES
CONV_SLAB = 1024

SLAB_HEADS = MXU_WIDTH // SSM_HEAD_DIM
SSM_SLABS = SSM_D_INNER // MXU_WIDTH
SLABS_PER_GROUP = SSM_SLABS // SSM_GROUPS
ATTN_PAIRS = ATTN_Q_DIM // LANES
PAIRS_PER_KV = ATTN_PAIRS // ATTN_KV_HEADS
KV_DUP_DIM = ATTN_KV_HEADS * LANES

COL_GATES = 0
COL_Z = COL_GATES + 2 * D_MODEL
COL_XBC = COL_Z + SSM_D_INNER
COL_QKV = COL_XBC + SSM_CONV_DIM
QKV_DIM = ATTN_Q_DIM + 2 * KV_DUP_DIM
COL_DT = COL_QKV + QKV_DIM
IN_COLS_PADDED = COL_DT + DT_PAD

FFN_ROWS = 512
MIX_ROWS = 256
VMEM_LIMIT = 56 * 1024 * 1024


def _rms(x, g):
    ms = jnp.mean(x * x, axis=-1, keepdims=True)
    return x * lax.rsqrt(ms + RMS_EPS) * g


def _sigmoid(x):
    return 1.0 / (1.0 + jnp.exp(-x))


def _silu(x):
    return x * _sigmoid(x)


def _softplus(x):
    return jnp.maximum(x, 0.0) + jnp.log1p(jnp.exp(-jnp.abs(x)))


def _dot(a, b):
    return jnp.dot(a, b, preferred_element_type=F32)


def _dot_nt(a, b):
    return lax.dot_general(a, b, (((1,), (1,)), ((), ())), preferred_element_type=F32)


def _split3(x):
    hi = x.astype(BF16)
    r1 = x - hi.astype(F32)
    mid = r1.astype(BF16)
    lo = (r1 - mid.astype(F32)).astype(BF16)
    return hi, mid, lo


def _resident(shape):
    return pl.BlockSpec(shape, lambda *_: (0,) * len(shape), pipeline_mode=pl.Buffered(1))


def _ffn_kernel(x_ref, pre_ref, wg_ref, wu_ref, wd_ref, post_ref, o_ref):
    x = x_ref[...]
    u = _rms(x, pre_ref[...]).astype(BF16)
    g = _dot(u, wg_ref[...])
    up = _dot(u, wu_ref[...])
    hmid = (_silu(g) * up).astype(BF16)
    f = _dot(hmid, wd_ref[...])
    o_ref[...] = x + FFN_RESIDUAL_WEIGHT * _rms(f, post_ref[...])


def _ffn(x2d, pre_g, wg, wu, wd, post_g):
    t = x2d.shape[0]
    row = pl.BlockSpec((FFN_ROWS, D_MODEL), lambda i: (i, 0))
    return pl.pallas_call(
        _ffn_kernel,
        grid=(t // FFN_ROWS,),
        in_specs=[row, _resident((1, D_MODEL)), _resident((D_MODEL, D_FF)), _resident((D_MODEL, D_FF)),
                  _resident((D_FF, D_MODEL)), _resident((1, D_MODEL))],
        out_specs=row,
        out_shape=jax.ShapeDtypeStruct((t, D_MODEL), F32),
        compiler_params=pltpu.CompilerParams(dimension_semantics=("arbitrary",),
                                             vmem_limit_bytes=VMEM_LIMIT),
        name="ffn",
    )(x2d, pre_g, wg, wu, wd, post_g)


def _inproj_kernel(seq, h_ref, g_ref, w_ref, convw_ref, convb_ref, dtb_ref,
                   gates_o, z_o, xbc_o, qkv_o, dt_o, xpad, tail):
    rows = h_ref.shape[0]
    i = pl.program_id(0)

    @pl.when((i * rows) % seq == 0)
    def _new_sequence():
        tail[...] = jnp.zeros_like(tail)

    u = _rms(h_ref[...], g_ref[...]).astype(BF16)

    def conv_slab(n):
        cols = slice(n * CONV_SLAB, (n + 1) * CONV_SLAB)
        buf = xpad.at[n % 2]
        buf[0:CONV_TAIL, :] = tail[:, cols]
        buf[CONV_TAIL:CONV_TAIL + rows, :] = _dot(u, w_ref[:, COL_XBC + n * CONV_SLAB:COL_XBC + (n + 1) * CONV_SLAB])
        conv = convb_ref[:, cols]
        for k in range(SSM_CONV):
            start = CONV_TAIL - (SSM_CONV - 1) + k
            conv = conv + convw_ref[k:k + 1, cols] * buf[start:start + rows, :]
        xbc_o[:, cols] = _silu(conv).astype(BF16)
        tail[:, cols] = buf[rows:rows + CONV_TAIL, :]

    conv_slab(0)
    gates_o[...] = _sigmoid(_dot(u, w_ref[:, COL_GATES:COL_Z])).astype(BF16)
    conv_slab(1)
    z_o[...] = _silu(_dot(u, w_ref[:, COL_Z:COL_XBC])).astype(BF16)
    conv_slab(2)
    qkv_o[...] = _dot(u, w_ref[:, COL_QKV:COL_DT]).astype(BF16)
    dt_o[...] = _softplus(_dot(u, w_ref[:, COL_DT:IN_COLS_PADDED]) + dtb_ref[...])


def _inproj(h2d, g, w, conv_w, conv_b, dt_bias, seq):
    t = h2d.shape[0]

    def row(width):
        return pl.BlockSpec((FFN_ROWS, width), lambda i: (i, 0))

    widths = (2 * D_MODEL, SSM_D_INNER, SSM_CONV_DIM, QKV_DIM, DT_PAD)
    dtypes = (BF16, BF16, BF16, BF16, F32)
    return pl.pallas_call(
        functools.partial(_inproj_kernel, seq),
        grid=(t // FFN_ROWS,),
        in_specs=[row(D_MODEL), _resident((1, D_MODEL)), _resident((D_MODEL, IN_COLS_PADDED)),
                  _resident((SSM_CONV, SSM_CONV_DIM)), _resident((1, SSM_CONV_DIM)), _resident((1, DT_PAD))],
        out_specs=[row(w_) for w_ in widths],
        out_shape=[jax.ShapeDtypeStruct((t, w_), d_) for w_, d_ in zip(widths, dtypes)],
        scratch_shapes=[pltpu.VMEM((2, CONV_TAIL + FFN_ROWS, CONV_SLAB), F32),
                        pltpu.VMEM((CONV_TAIL, SSM_CONV_DIM), F32)],
        compiler_params=pltpu.CompilerParams(dimension_semantics=("arbitrary",),
                                             vmem_limit_bytes=VMEM_LIMIT),
        name="inproj",
    )(h2d, g, w, conv_w, conv_b, dt_bias)


def _mixer_kernel(tiles_per_seq, sinks_ref, table_ref,
                  h_ref, gates_ref, z_ref, xbc_ref, qkv_ref, dt_ref,
                  alog_ref, dexp_ref, ng_ref, bucket_ref,
                  wssm_ref, wattn_ref, wout_ref, postg_ref,
                  o_ref,
                  kvpad, yscr, ynorm, attn, pssm, pmerged, pmix, state, bias):
    rows_per_step = xbc_ref.shape[0]
    n_chunks = rows_per_step // SSM_CHUNK
    s = pl.program_id(0)
    cur = s % 2
    prev = 1 - cur
    q = SSM_CHUNK

    ii = lax.broadcasted_iota(jnp.int32, (q, q), 0)
    jj = lax.broadcasted_iota(jnp.int32, (q, q), 1)
    causal = ii >= jj
    tril3 = jnp.concatenate([jnp.where(causal, 1.0, 0.0).astype(BF16)] * 3, axis=1)
    low_half = jj < SSM_HEAD_DIM
    low_half_row = low_half[0:1, :]
    slab_lane_head = lax.broadcasted_iota(jnp.int32, (q, MXU_WIDTH), 1) // SSM_HEAD_DIM
    band_low_half = lax.broadcasted_iota(jnp.int32, (2 * ATTN_BLOCK, LANES), 1) < ATTN_HEAD_DIM

    @pl.when(s == 0)
    def _init():
        bi = lax.broadcasted_iota(jnp.int32, (ATTN_BLOCK, 2 * ATTN_BLOCK), 0)
        bj = lax.broadcasted_iota(jnp.int32, (ATTN_BLOCK, 2 * ATTN_BLOCK), 1)
        dist = bi + ATTN_BLOCK - bj
        in_window = (dist >= 0) & (dist < ATTN_WINDOW)
        bucket = bucket_ref[...]
        for hq in range(ATTN_Q_HEADS):
            acc = jnp.zeros((ATTN_BLOCK, 2 * ATTN_BLOCK), F32)
            for bk in range(REL_BUCKETS):
                acc = jnp.where(bucket == bk, table_ref[bk, hq], acc)
            masked = jnp.where(in_window, acc, -jnp.inf)
            bias[0, hq] = masked
            bias[1, hq] = jnp.where(bj >= ATTN_BLOCK, masked, -jnp.inf)
        ynorm[1] = jnp.zeros((rows_per_step, SSM_D_INNER), BF16)
        attn[1] = jnp.zeros((rows_per_step, ATTN_Q_DIM), BF16)

    new_sequence = s % tiles_per_seq == 0

    @pl.when(new_sequence)
    def _reset_sequence_state():
        state[...] = jnp.zeros_like(state)
        kvpad[0:ATTN_BLOCK, :] = jnp.zeros((ATTN_BLOCK, 2 * KV_DUP_DIM), BF16)

    kvpad[ATTN_BLOCK:ATTN_BLOCK + rows_per_step, :] = qkv_ref[:, ATTN_Q_DIM:QKV_DIM]

    a_neg = -jnp.exp(alog_ref[...])

    pending = []
    for n in range(D_MODEL // MXU_WIDTH):
        ns = slice(n * MXU_WIDTH, (n + 1) * MXU_WIDTH)
        ns_attn = slice(D_MODEL + n * MXU_WIDTH, D_MODEL + (n + 1) * MXU_WIDTH)

        def ssm_item(ns=ns):
            pssm[:, ns] = _dot(ynorm[prev], wssm_ref[:, ns])

        def attn_merge_item(ns=ns, ns_attn=ns_attn):
            y_attn = _dot(attn[prev], wattn_ref[:, ns])
            merged = gates_ref[:, ns].astype(F32) * pssm[:, ns] + gates_ref[:, ns_attn].astype(F32) * y_attn
            pmerged[:, ns] = merged.astype(BF16)

        pending += [ssm_item, attn_merge_item]
    for n in range(D_MODEL // MXU_WIDTH):
        ns = slice(n * MXU_WIDTH, (n + 1) * MXU_WIDTH)

        def out_item(ns=ns):
            pmix[:, ns] = _dot(pmerged[...], wout_ref[:, ns])

        pending.append(out_item)

    def residual_item():
        o_ref[...] = h_ref[...] + _rms(pmix[...], postg_ref[...])

    pending.append(residual_item)
    n_slots = n_chunks * 2 * SSM_SLABS
    assert len(pending) <= n_slots
    issue_at = {(k * n_slots) // len(pending) for k in range(len(pending))}
    slot = [0]

    def issue_pending():
        if slot[0] in issue_at:
            pending.pop(0)()
        slot[0] += 1

    for c in range(n_chunks):
        rows = slice(c * q, (c + 1) * q)

        dt = dt_ref[rows, :]
        hi, mid, lo = _split3(dt * a_neg)
        acs2 = _dot(tril3, jnp.concatenate([hi, mid, lo], axis=0)) * LOG2E
        row2_t = (acs2 - jnp.log2(dt)).T

        first = jnp.where(new_sequence, 1, 0) if c == 0 else 0
        band = slice(c * q, c * q + 2 * ATTN_BLOCK)

        def attention_logits(pair):
            kv = pair // PAIRS_PER_KV
            kd = kvpad[band, kv * LANES:(kv + 1) * LANES]
            zero = jnp.zeros_like(kd)
            k_cat = jnp.concatenate([jnp.where(band_low_half, kd, zero), jnp.where(band_low_half, zero, kd)], axis=0)
            q_pair = qkv_ref[rows, pair * LANES:(pair + 1) * LANES]
            return _dot_nt(q_pair, k_cat)

        def attention_output(pair, logits):
            kv = pair // PAIRS_PER_KV
            vd = kvpad[band, KV_DUP_DIM + kv * LANES:KV_DUP_DIM + (kv + 1) * LANES]
            zero = jnp.zeros_like(vd)
            v_cat = jnp.concatenate([jnp.where(band_low_half, vd, zero), jnp.where(band_low_half, zero, vd)], axis=0)
            ps, invs = [], []
            for half in range(2):
                hq = 2 * pair + half
                l2 = logits[:, half * 2 * ATTN_BLOCK:(half + 1) * 2 * ATTN_BLOCK] + bias[first, hq]
                sink = sinks_ref[hq]
                mx = jnp.maximum(jnp.max(l2, axis=-1, keepdims=True), sink)
                p = jnp.exp2(l2 - mx)
                den = jnp.sum(p, axis=-1, keepdims=True) + jnp.exp2(sink - mx)
                ps.append(p.astype(BF16))
                invs.append(1.0 / den)
            o = _dot(jnp.concatenate(ps, axis=1), v_cat)
            o = o * jnp.where(low_half, invs[0], invs[1])
            attn[cur, rows, pair * LANES:(pair + 1) * LANES] = o.astype(BF16)

        logits_next = attention_logits(0)
        for m in range(SSM_SLABS):
            logits_m = logits_next
            issue_pending()
            g = m // SLABS_PER_GROUP
            if m % SLABS_PER_GROUP == 0:
                b_g = xbc_ref[rows, SSM_D_INNER + g * SSM_STATE:SSM_D_INNER + (g + 1) * SSM_STATE]
                c_g = xbc_ref[rows, SSM_D_INNER + SSM_BC_DIM + g * SSM_STATE:
                              SSM_D_INNER + SSM_BC_DIM + (g + 1) * SSM_STATE]
                scores = _dot_nt(c_g, b_g)
                b_gt = b_g.astype(F32).T
            h_prev = state[m]
            y_off = _dot(c_g, h_prev.astype(BF16))
            if m + 1 < ATTN_PAIRS:
                logits_next = attention_logits(m + 1)
            tops, bots, ebs, cds = [], [], [], []
            for r in range(SLAB_HEADS):
                h = m * SLAB_HEADS + r
                colb = jnp.broadcast_to(acs2[:, h:h + 1], (q, q))
                row2 = row2_t[h:h + 1, :]
                lastb = colb[q - 1:q, :]
                l_mat = jnp.exp2(jnp.where(causal, colb - row2, -jnp.inf)) * scores
                tops.append(l_mat.astype(BF16))
                bots.append((b_gt * jnp.exp2(lastb - row2)).astype(BF16))
                ebs.append(jnp.exp2(colb))
                cds.append(jnp.exp2(lastb))
            lhs = jnp.concatenate([jnp.concatenate(tops, axis=1), jnp.concatenate(bots, axis=1)], axis=0)
            xs_slab = xbc_ref[rows, m * MXU_WIDTH:(m + 1) * MXU_WIDTH]
            rhs = jnp.concatenate(
                [jnp.where(slab_lane_head == r, xs_slab, jnp.zeros_like(xs_slab)) for r in range(SLAB_HEADS)],
                axis=0)
            yy = _dot(lhs, rhs)
            e_slab = jnp.concatenate([jnp.where(low_half, ebs[0], ebs[1]),
                                      jnp.where(low_half, ebs[2], ebs[3])], axis=1)
            cd_slab = jnp.concatenate([jnp.where(low_half_row, cds[0], cds[1]),
                                       jnp.where(low_half_row, cds[2], cds[3])], axis=1)
            yscr[:, m * MXU_WIDTH:(m + 1) * MXU_WIDTH] = yy[0:q] + y_off * e_slab
            state[m] = h_prev * cd_slab + yy[q:2 * q]
            issue_pending()
            attention_output(m, logits_m)

        xs = xbc_ref[rows, 0:SSM_D_INNER].astype(F32)
        y = yscr[...] + dexp_ref[...] * xs
        yg = y * z_ref[rows, :].astype(F32)
        for g in range(SSM_GROUPS):
            gl = slice(g * SSM_GROUP_WIDTH, (g + 1) * SSM_GROUP_WIDTH)
            v = yg[:, gl]
            ms = jnp.mean(v * v, axis=-1, keepdims=True)
            ynorm[cur, rows, gl] = (v * lax.rsqrt(ms + RMS_EPS) * ng_ref[:, gl]).astype(BF16)

    assert not pending

    kvpad[0:ATTN_BLOCK, :] = kvpad[rows_per_step:rows_per_step + ATTN_BLOCK, :]


def _t5_bucket_matrix():
    qi = np.arange(ATTN_BLOCK)[:, None]
    kj = np.arange(2 * ATTN_BLOCK)[None, :]
    dist = np.maximum(qi + ATTN_BLOCK - kj, 0).astype(np.int32)
    max_exact = REL_BUCKETS // 2
    d = np.maximum(dist, 1).astype(np.float32)
    large = max_exact + (np.log(d / np.float32(max_exact)) / np.float32(math.log(REL_MAX_DISTANCE / max_exact))
                         * np.float32(REL_BUCKETS - max_exact)).astype(np.int32)
    large = np.minimum(large, REL_BUCKETS - 1)
    return np.where(dist < max_exact, dist, large).astype(np.int32)


def _mixer(h2d, gates, z, xbc, qkv, dt, seq, a_log, d_exp, norm_g, w_ssm, w_attn, w_out, post_g, sinks, table):
    n_tiles = h2d.shape[0] // MIX_ROWS

    def scanned(width):
        return pl.BlockSpec((MIX_ROWS, width), lambda s: (jnp.minimum(s, n_tiles - 1), 0))

    def finished(width):
        return pl.BlockSpec((MIX_ROWS, width), lambda s: (jnp.maximum(s - 1, 0), 0))

    smem = pl.BlockSpec(memory_space=pltpu.SMEM)
    bucket = jnp.asarray(_t5_bucket_matrix())
    in_specs = [
        smem, smem,
        finished(D_MODEL), finished(2 * D_MODEL),
        scanned(SSM_D_INNER), scanned(SSM_CONV_DIM), scanned(QKV_DIM), scanned(DT_PAD),
        _resident((1, DT_PAD)), _resident((1, SSM_D_INNER)), _resident((1, SSM_D_INNER)),
        _resident((ATTN_BLOCK, 2 * ATTN_BLOCK)),
        _resident((SSM_D_INNER, D_MODEL)), _resident((ATTN_Q_DIM, D_MODEL)), _resident((D_MODEL, D_MODEL)),
        _resident((1, D_MODEL)),
    ]
    scratch = [
        pltpu.VMEM((ATTN_BLOCK + MIX_ROWS, 2 * KV_DUP_DIM), BF16),
        pltpu.VMEM((SSM_CHUNK, SSM_D_INNER), F32),
        pltpu.VMEM((2, MIX_ROWS, SSM_D_INNER), BF16),
        pltpu.VMEM((2, MIX_ROWS, ATTN_Q_DIM), BF16),
        pltpu.VMEM((MIX_ROWS, D_MODEL), F32),
        pltpu.VMEM((MIX_ROWS, D_MODEL), BF16),
        pltpu.VMEM((MIX_ROWS, D_MODEL), F32),
        pltpu.VMEM((SSM_SLABS, SSM_STATE, MXU_WIDTH), F32),
        pltpu.VMEM((2, ATTN_Q_HEADS, ATTN_BLOCK, 2 * ATTN_BLOCK), F32),
    ]
    return pl.pallas_call(
        functools.partial(_mixer_kernel, seq // MIX_ROWS),
        grid=(n_tiles + 1,),
        in_specs=in_specs,
        out_specs=finished(D_MODEL),
        out_shape=jax.ShapeDtypeStruct(h2d.shape, F32),
        scratch_shapes=scratch,
        compiler_params=pltpu.CompilerParams(dimension_semantics=("arbitrary",),
                                             vmem_limit_bytes=VMEM_LIMIT),
        name="mixer",
    )(sinks, table, h2d, gates, z, xbc, qkv, dt, a_log, d_exp, norm_g, bucket,
      w_ssm, w_attn, w_out, post_g)


def _reorder_w_in(w_in):
    o_z = 2 * D_MODEL
    o_dt = o_z + SSM_D_INNER + SSM_CONV_DIM
    o_q = o_dt + SSM_HEADS
    o_k = o_q + ATTN_Q_DIM
    o_v = o_k + ATTN_KV_DIM

    def dup_heads(w):
        w = w.reshape(D_MODEL, ATTN_KV_HEADS, 1, ATTN_HEAD_DIM)
        return jnp.broadcast_to(w, (D_MODEL, ATTN_KV_HEADS, 2, ATTN_HEAD_DIM)).reshape(D_MODEL, KV_DUP_DIM)

    w_q = w_in[:, o_q:o_k] * (ATTN_HEAD_DIM ** -0.5 * LOG2E)
    dt_cols = jnp.pad(w_in[:, o_dt:o_q], ((0, 0), (0, DT_PAD - SSM_HEADS)))
    return jnp.concatenate([w_in[:, :o_dt], w_q, dup_heads(w_in[:, o_k:o_v]), dup_heads(w_in[:, o_v:]), dt_cols],
                           axis=1)


def _pad_heads(v):
    return jnp.pad(v.astype(F32), (0, DT_PAD - SSM_HEADS))[None, :]


def kernel(x, ffn1_pre_g, ffn1_w_gate, ffn1_w_up, ffn1_w_down, ffn1_post_g, mix_pre_g, w_in, conv_w, conv_b, dt_bias, a_log, d_skip, ssm_norm_g, w_ssm_proj, attn_sinks, rel_bias_table, w_attn_proj, w_out, mix_post_g, ffn2_pre_g, ffn2_w_gate, ffn2_w_up, ffn2_w_down, ffn2_post_g):
    batch, seq, _ = x.shape
    depth = w_in.shape[0]
    h = x.reshape(batch * seq, D_MODEL)
    table = rel_bias_table.astype(F32) * LOG2E
    for l in range(depth):
        h = _ffn(h, ffn1_pre_g[l][None, :], ffn1_w_gate[l].astype(BF16), ffn1_w_up[l].astype(BF16),
                 ffn1_w_down[l].astype(BF16), ffn1_post_g[l][None, :])
        gates, z, xbc, qkv, dt = _inproj(h, mix_pre_g[l][None, :], _reorder_w_in(w_in[l]).astype(BF16),
                                         conv_w[l].astype(F32), conv_b[l].astype(F32)[None, :],
                                         _pad_heads(dt_bias[l]), seq)
        h = _mixer(h, gates, z, xbc, qkv, dt, seq, _pad_heads(a_log[l]),
                   jnp.repeat(d_skip[l].astype(F32), SSM_HEAD_DIM)[None, :], ssm_norm_g[l][None, :],
                   w_ssm_proj[l].astype(BF16), w_attn_proj[l].astype(BF16), w_out[l].astype(BF16),
                   mix_post_g[l][None, :], attn_sinks[l].astype(F32) * LOG2E, table)
        h = _ffn(h, ffn2_pre_g[l][None, :], ffn2_w_gate[l].astype(BF16), ffn2_w_up[l].astype(BF16),
                 ffn2_w_down[l].astype(BF16), ffn2_post_g[l][None, :])
    return h.reshape(batch, seq, D_MODEL)
```

```python
import functools
import math

import numpy as np
import jax
import jax.numpy as jnp
from jax import lax
from jax.experimental import pallas as pl
from jax.experimental.pallas import tpu as pltpu

F32 = jnp.float32
BF16 = jnp.bfloat16

D_MODEL = 1024
D_FF = 2816
FFN_RESIDUAL_WEIGHT = 0.5
SSM_D_INNER = 2 * D_MODEL
SSM_HEAD_DIM = 64
SSM_HEADS = SSM_D_INNER // SSM_HEAD_DIM
SSM_GROUPS = 4
SSM_STATE = 128
SSM_CONV = 4
SSM_CHUNK = 128
SSM_BC_DIM = SSM_GROUPS * SSM_STATE
SSM_CONV_DIM = SSM_D_INNER + 2 * SSM_BC_DIM
SSM_GROUP_WIDTH = SSM_D_INNER // SSM_GROUPS
ATTN_Q_HEADS = 16
ATTN_KV_HEADS = 4
ATTN_HEAD_DIM = 64
ATTN_WINDOW = 128
ATTN_BLOCK = 128
ATTN_Q_DIM = ATTN_Q_HEADS * ATTN_HEAD_DIM
ATTN_KV_DIM = ATTN_KV_HEADS * ATTN_HEAD_DIM
REL_BUCKETS = 32
REL_MAX_DISTANCE = 128
RMS_EPS = 1e-6
LOG2E = math.log2(math.e)

LANES = 128
SUBLANES = 8
MXU_WIDTH = 256
DT_PAD = LANES
CONV_TAIL = SUBLANES
CONV_SLAB = MXU_WIDTH
CONV_BUFFERS = 4

SLAB_HEADS = MXU_WIDTH // SSM_HEAD_DIM
SSM_SLABS = SSM_D_INNER // MXU_WIDTH
SLABS_PER_GROUP = SSM_SLABS // SSM_GROUPS
ATTN_PAIRS = ATTN_Q_DIM // LANES
PAIRS_PER_KV = ATTN_PAIRS // ATTN_KV_HEADS
KV_DUP_DIM = ATTN_KV_HEADS * LANES

COL_GATES = 0
COL_Z = COL_GATES + 2 * D_MODEL
COL_XBC = COL_Z + SSM_D_INNER
COL_QKV = COL_XBC + SSM_CONV_DIM
QKV_DIM = ATTN_Q_DIM + 2 * KV_DUP_DIM
COL_DT = COL_QKV + QKV_DIM
IN_COLS_PADDED = COL_DT + DT_PAD

FFN_ROWS = 512
MIX_ROWS = 256
VMEM_LIMIT = 56 * 1024 * 1024


def _rms(x, g):
    ms = jnp.mean(x * x, axis=-1, keepdims=True)
    return x * lax.rsqrt(ms + RMS_EPS) * g


def _sigmoid(x):
    return 0.5 * jnp.tanh(0.5 * x) + 0.5


def _silu(x):
    return x * _sigmoid(x)


def _softplus(x):
    return jnp.maximum(x, 0.0) + jnp.log1p(jnp.exp(-jnp.abs(x)))


def _dot(a, b):
    return jnp.dot(a, b, preferred_element_type=F32)


def _dot_nt(a, b):
    return lax.dot_general(a, b, (((1,), (1,)), ((), ())), preferred_element_type=F32)


def _split3(x):
    hi = x.astype(BF16)
    r1 = x - hi.astype(F32)
    mid = r1.astype(BF16)
    lo = (r1 - mid.astype(F32)).astype(BF16)
    return hi, mid, lo


def _resident(shape):
    return pl.BlockSpec(shape, lambda *_: (0,) * len(shape), pipeline_mode=pl.Buffered(1))


def _ffn_kernel(x_ref, pre_ref, wg_ref, wu_ref, wd_ref, post_ref, o_ref):
    x = x_ref[...]
    u = _rms(x, pre_ref[...]).astype(BF16)
    g = _dot(u, wg_ref[...])
    up = _dot(u, wu_ref[...])
    hmid = (_silu(g) * up).astype(BF16)
    f = _dot(hmid, wd_ref[...])
    o_ref[...] = x + FFN_RESIDUAL_WEIGHT * _rms(f, post_ref[...])


def _ffn(x2d, pre_g, wg, wu, wd, post_g):
    t = x2d.shape[0]
    row = pl.BlockSpec((FFN_ROWS, D_MODEL), lambda i: (i, 0))
    return pl.pallas_call(
        _ffn_kernel,
        grid=(t // FFN_ROWS,),
        in_specs=[row, _resident((1, D_MODEL)), _resident((D_MODEL, D_FF)), _resident((D_MODEL, D_FF)),
                  _resident((D_FF, D_MODEL)), _resident((1, D_MODEL))],
        out_specs=row,
        out_shape=jax.ShapeDtypeStruct((t, D_MODEL), F32),
        compiler_params=pltpu.CompilerParams(dimension_semantics=("arbitrary",),
                                             vmem_limit_bytes=VMEM_LIMIT),
        name="ffn",
    )(x2d, pre_g, wg, wu, wd, post_g)


def _inproj_kernel(seq, h_ref, g_ref, w_ref, convw_ref, convb_ref, dtb_ref,
                   gates_o, z_o, xbc_o, qkv_o, dt_o, xpad, tail):
    rows = h_ref.shape[0]
    i = pl.program_id(0)

    @pl.when((i * rows) % seq == 0)
    def _new_sequence():
        tail[...] = jnp.zeros_like(tail)

    u = _rms(h_ref[...], g_ref[...]).astype(BF16)

    def conv_slab(n):
        cols = slice(n * CONV_SLAB, (n + 1) * CONV_SLAB)
        buf = xpad.at[n % CONV_BUFFERS]
        buf[0:CONV_TAIL, :] = tail[:, cols]
        buf[CONV_TAIL:CONV_TAIL + rows, :] = _dot(u, w_ref[:, COL_XBC + n * CONV_SLAB:COL_XBC + (n + 1) * CONV_SLAB])
        conv = convb_ref[:, cols]
        for k in range(SSM_CONV):
            start = CONV_TAIL - (SSM_CONV - 1) + k
            conv = conv + convw_ref[k:k + 1, cols] * buf[start:start + rows, :]
        xbc_o[:, cols] = _silu(conv).astype(BF16)
        tail[:, cols] = buf[rows:rows + CONV_TAIL, :]

    def piece(out_ref, col0, n, act):
        cols = slice(n * CONV_SLAB, (n + 1) * CONV_SLAB)
        out_ref[:, cols] = act(_dot(u, w_ref[:, col0 + n * CONV_SLAB:col0 + (n + 1) * CONV_SLAB])).astype(BF16)

    def dt_piece():
        dt_o[...] = _softplus(_dot(u, w_ref[:, COL_DT:IN_COLS_PADDED]) + dtb_ref[...])

    others = []
    for n in range(QKV_DIM // CONV_SLAB):
        others.append(functools.partial(piece, gates_o, COL_GATES, n, _sigmoid))
        others.append(functools.partial(piece, qkv_o, COL_QKV, n, lambda v: v))
        others.append(functools.partial(piece, z_o, COL_Z, n, _silu))
    others.append(dt_piece)
    n_conv = SSM_CONV_DIM // CONV_SLAB
    for n in range(n_conv):
        conv_slab(n)
        for k in range(n * len(others) // n_conv, (n + 1) * len(others) // n_conv):
            others[k]()


def _inproj(h2d, g, w, conv_w, conv_b, dt_bias, seq):
    t = h2d.shape[0]

    def row(width):
        return pl.BlockSpec((FFN_ROWS, width), lambda i: (i, 0))

    widths = (2 * D_MODEL, SSM_D_INNER, SSM_CONV_DIM, QKV_DIM, DT_PAD)
    dtypes = (BF16, BF16, BF16, BF16, F32)
    return pl.pallas_call(
        functools.partial(_inproj_kernel, seq),
        grid=(t // FFN_ROWS,),
        in_specs=[row(D_MODEL), _resident((1, D_MODEL)), _resident((D_MODEL, IN_COLS_PADDED)),
                  _resident((SSM_CONV, SSM_CONV_DIM)), _resident((1, SSM_CONV_DIM)), _resident((1, DT_PAD))],
        out_specs=[row(w_) for w_ in widths],
        out_shape=[jax.ShapeDtypeStruct((t, w_), d_) for w_, d_ in zip(widths, dtypes)],
        scratch_shapes=[pltpu.VMEM((CONV_BUFFERS, CONV_TAIL + FFN_ROWS, CONV_SLAB), F32),
                        pltpu.VMEM((CONV_TAIL, SSM_CONV_DIM), F32)],
        compiler_params=pltpu.CompilerParams(dimension_semantics=("arbitrary",),
                                             vmem_limit_bytes=VMEM_LIMIT),
        name="inproj",
    )(h2d, g, w, conv_w, conv_b, dt_bias)


def _mixer_kernel(tiles_per_seq, sinks_ref, table_ref,
                  h_ref, gates_ref, z_ref, xbc_ref, qkv_ref, dt_ref,
                  alog_ref, dexp_ref, ng_ref, bucket_ref,
                  wssm_ref, wattn_ref, wout_ref, postg_ref,
                  o_ref,
                  kvpad, yscr, ynorm, attn, pssm, pmerged, pmix, state, bias):
    rows_per_step = xbc_ref.shape[0]
    n_chunks = rows_per_step // SSM_CHUNK
    s = pl.program_id(0)
    cur = s % 2
    prev = 1 - cur
    q = SSM_CHUNK

    ii = lax.broadcasted_iota(jnp.int32, (q, q), 0)
    jj = lax.broadcasted_iota(jnp.int32, (q, q), 1)
    causal = ii >= jj
    tril3 = jnp.concatenate([jnp.where(causal, 1.0, 0.0).astype(BF16)] * 3, axis=1)
    low_half = jj < SSM_HEAD_DIM
    low_half_row = low_half[0:1, :]
    slab_lane_head = lax.broadcasted_iota(jnp.int32, (q, MXU_WIDTH), 1) // SSM_HEAD_DIM
    band_low_half = lax.broadcasted_iota(jnp.int32, (2 * ATTN_BLOCK, LANES), 1) < ATTN_HEAD_DIM

    @pl.when(s == 0)
    def _init():
        bi = lax.broadcasted_iota(jnp.int32, (ATTN_BLOCK, 2 * ATTN_BLOCK), 0)
        bj = lax.broadcasted_iota(jnp.int32, (ATTN_BLOCK, 2 * ATTN_BLOCK), 1)
        dist = bi + ATTN_BLOCK - bj
        in_window = (dist >= 0) & (dist < ATTN_WINDOW)
        bucket = bucket_ref[...]
        for hq in range(ATTN_Q_HEADS):
            acc = jnp.zeros((ATTN_BLOCK, 2 * ATTN_BLOCK), F32)
            for bk in range(REL_BUCKETS):
                acc = jnp.where(bucket == bk, table_ref[bk, hq], acc)
            masked = jnp.where(in_window, acc, -jnp.inf)
            bias[0, hq] = masked
            bias[1, hq] = jnp.where(bj >= ATTN_BLOCK, masked, -jnp.inf)
        ynorm[1] = jnp.zeros((rows_per_step, SSM_D_INNER), BF16)
        attn[1] = jnp.zeros((rows_per_step, ATTN_Q_DIM), BF16)

    new_sequence = s % tiles_per_seq == 0

    @pl.when(new_sequence)
    def _reset_sequence_state():
        state[...] = jnp.zeros_like(state)
        kvpad[0:ATTN_BLOCK, :] = jnp.zeros((ATTN_BLOCK, 2 * KV_DUP_DIM), BF16)

    kvpad[ATTN_BLOCK:ATTN_BLOCK + rows_per_step, :] = qkv_ref[:, ATTN_Q_DIM:QKV_DIM]

    a_neg = -jnp.exp(alog_ref[...])

    pending = []
    for n in range(D_MODEL // MXU_WIDTH):
        ns = slice(n * MXU_WIDTH, (n + 1) * MXU_WIDTH)
        ns_attn = slice(D_MODEL + n * MXU_WIDTH, D_MODEL + (n + 1) * MXU_WIDTH)

        def ssm_item(ns=ns):
            pssm[:, ns] = _dot(ynorm[prev], wssm_ref[:, ns])

        def attn_merge_item(ns=ns, ns_attn=ns_attn):
            y_attn = _dot(attn[prev], wattn_ref[:, ns])
            merged = gates_ref[:, ns].astype(F32) * pssm[:, ns] + gates_ref[:, ns_attn].astype(F32) * y_attn
            pmerged[:, ns] = merged.astype(BF16)

        pending += [ssm_item, attn_merge_item]
    for n in range(D_MODEL // MXU_WIDTH):
        ns = slice(n * MXU_WIDTH, (n + 1) * MXU_WIDTH)

        def out_item(ns=ns):
            pmix[:, ns] = _dot(pmerged[...], wout_ref[:, ns])

        pending.append(out_item)

    def residual_item():
        o_ref[...] = h_ref[...] + _rms(pmix[...], postg_ref[...])

    pending.append(residual_item)
    n_slots = n_chunks * 2 * SSM_SLABS
    assert len(pending) <= n_slots
    issue_at = {(k * n_slots) // len(pending) for k in range(len(pending))}
    slot = [0]

    def issue_pending():
        if slot[0] in issue_at:
            pending.pop(0)()
        slot[0] += 1

    for c in range(n_chunks):
        rows = slice(c * q, (c + 1) * q)

        dt = dt_ref[rows, :]
        hi, mid, lo = _split3(dt * a_neg)
        acs2 = _dot(tril3, jnp.concatenate([hi, mid, lo], axis=0)) * LOG2E
        row2_t = (acs2 - jnp.log2(dt)).T

        first = jnp.where(new_sequence, 1, 0) if c == 0 else 0
        band = slice(c * q, c * q + 2 * ATTN_BLOCK)

        def attention_logits(pair):
            kv = pair // PAIRS_PER_KV
            kd = kvpad[band, kv * LANES:(kv + 1) * LANES]
            zero = jnp.zeros_like(kd)
            k_cat = jnp.concatenate([jnp.where(band_low_half, kd, zero), jnp.where(band_low_half, zero, kd)], axis=0)
            q_pair = qkv_ref[rows, pair * LANES:(pair + 1) * LANES]
            return _dot_nt(q_pair, k_cat)

        def attention_output(pair, logits):
            kv = pair // PAIRS_PER_KV
            vd = kvpad[band, KV_DUP_DIM + kv * LANES:KV_DUP_DIM + (kv + 1) * LANES]
            zero = jnp.zeros_like(vd)
            v_cat = jnp.concatenate([jnp.where(band_low_half, vd, zero), jnp.where(band_low_half, zero, vd)], axis=0)
            ps, invs = [], []
            for half in range(2):
                hq = 2 * pair + half
                l2 = logits[:, half * 2 * ATTN_BLOCK:(half + 1) * 2 * ATTN_BLOCK] + bias[first, hq]
                sink = sinks_ref[hq]
                mx = jnp.maximum(jnp.max(l2, axis=-1, keepdims=True), sink)
                p = jnp.exp2(l2 - mx)
                den = jnp.sum(p, axis=-1, keepdims=True) + jnp.exp2(sink - mx)
                ps.append(p.astype(BF16))
                invs.append(1.0 / den)
            o = _dot(jnp.concatenate(ps, axis=1), v_cat)
            o = o * jnp.where(low_half, invs[0], invs[1])
            attn[cur, rows, pair * LANES:(pair + 1) * LANES] = o.astype(BF16)

        logits_next = attention_logits(0)
        for m in range(SSM_SLABS):
            logits_m = logits_next
            issue_pending()
            g = m // SLABS_PER_GROUP
            if m % SLABS_PER_GROUP == 0:
                b_g = xbc_ref[rows, SSM_D_INNER + g * SSM_STATE:SSM_D_INNER + (g + 1) * SSM_STATE]
                c_g = xbc_ref[rows, SSM_D_INNER + SSM_BC_DIM + g * SSM_STATE:
                              SSM_D_INNER + SSM_BC_DIM + (g + 1) * SSM_STATE]
                scores = _dot_nt(c_g, b_g)
                b_gt = b_g.astype(F32).T
            h_prev = state[m]
            y_off = _dot(c_g, h_prev.astype(BF16))
            if m + 1 < ATTN_PAIRS:
                logits_next = attention_logits(m + 1)
            tops, bots, ebs, cds = [], [], [], []
            for r in range(SLAB_HEADS):
                h = m * SLAB_HEADS + r
                colb = jnp.broadcast_to(acs2[:, h:h + 1], (q, q))
                row2 = row2_t[h:h + 1, :]
                lastb = colb[q - 1:q, :]
                l_mat = jnp.exp2(jnp.where(causal, colb - row2, -jnp.inf)) * scores
                tops.append(l_mat.astype(BF16))
                bots.append((b_gt * jnp.exp2(lastb - row2)).astype(BF16))
                ebs.append(jnp.exp2(colb))
                cds.append(jnp.exp2(lastb))
            lhs = jnp.concatenate([jnp.concatenate(tops, axis=1), jnp.concatenate(bots, axis=1)], axis=0)
            xs_slab = xbc_ref[rows, m * MXU_WIDTH:(m + 1) * MXU_WIDTH]
            rhs = jnp.concatenate(
                [jnp.where(slab_lane_head == r, xs_slab, jnp.zeros_like(xs_slab)) for r in range(SLAB_HEADS)],
                axis=0)
            yy = _dot(lhs, rhs)
            e_slab = jnp.concatenate([jnp.where(low_half, ebs[0], ebs[1]),
                                      jnp.where(low_half, ebs[2], ebs[3])], axis=1)
            cd_slab = jnp.concatenate([jnp.where(low_half_row, cds[0], cds[1]),
                                       jnp.where(low_half_row, cds[2], cds[3])], axis=1)
            yscr[:, m * MXU_WIDTH:(m + 1) * MXU_WIDTH] = yy[0:q] + y_off * e_slab
            state[m] = h_prev * cd_slab + yy[q:2 * q]
            issue_pending()
            attention_output(m, logits_m)

        xs = xbc_ref[rows, 0:SSM_D_INNER].astype(F32)
        y = yscr[...] + dexp_ref[...] * xs
        yg = y * z_ref[rows, :].astype(F32)
        for g in range(SSM_GROUPS):
            gl = slice(g * SSM_GROUP_WIDTH, (g + 1) * SSM_GROUP_WIDTH)
            v = yg[:, gl]
            ms = jnp.mean(v * v, axis=-1, keepdims=True)
            ynorm[cur, rows, gl] = (v * lax.rsqrt(ms + RMS_EPS) * ng_ref[:, gl]).astype(BF16)

    assert not pending

    kvpad[0:ATTN_BLOCK, :] = kvpad[rows_per_step:rows_per_step + ATTN_BLOCK, :]


def _t5_bucket_matrix():
    qi = np.arange(ATTN_BLOCK)[:, None]
    kj = np.arange(2 * ATTN_BLOCK)[None, :]
    dist = np.maximum(qi + ATTN_BLOCK - kj, 0).astype(np.int32)
    max_exact = REL_BUCKETS // 2
    d = np.maximum(dist, 1).astype(np.float32)
    large = max_exact + (np.log(d / np.float32(max_exact)) / np.float32(math.log(REL_MAX_DISTANCE / max_exact))
                         * np.float32(REL_BUCKETS - max_exact)).astype(np.int32)
    large = np.minimum(large, REL_BUCKETS - 1)
    return np.where(dist < max_exact, dist, large).astype(np.int32)


def _mixer(h2d, gates, z, xbc, qkv, dt, seq, a_log, d_exp, norm_g, w_ssm, w_attn, w_out, post_g, sinks, table):
    n_tiles = h2d.shape[0] // MIX_ROWS

    def scanned(width):
        return pl.BlockSpec((MIX_ROWS, width), lambda s: (jnp.minimum(s, n_tiles - 1), 0))

    def finished(width):
        return pl.BlockSpec((MIX_ROWS, width), lambda s: (jnp.maximum(s - 1, 0), 0))

    smem = pl.BlockSpec(memory_space=pltpu.SMEM)
    bucket = jnp.asarray(_t5_bucket_matrix())
    in_specs = [
        smem, smem,
        finished(D_MODEL), finished(2 * D_MODEL),
        scanned(SSM_D_INNER), scanned(SSM_CONV_DIM), scanned(QKV_DIM), scanned(DT_PAD),
        _resident((1, DT_PAD)), _resident((1, SSM_D_INNER)), _resident((1, SSM_D_INNER)),
        _resident((ATTN_BLOCK, 2 * ATTN_BLOCK)),
        _resident((SSM_D_INNER, D_MODEL)), _resident((ATTN_Q_DIM, D_MODEL)), _resident((D_MODEL, D_MODEL)),
        _resident((1, D_MODEL)),
    ]
    scratch = [
        pltpu.VMEM((ATTN_BLOCK + MIX_ROWS, 2 * KV_DUP_DIM), BF16),
        pltpu.VMEM((SSM_CHUNK, SSM_D_INNER), F32),
        pltpu.VMEM((2, MIX_ROWS, SSM_D_INNER), BF16),
        pltpu.VMEM((2, MIX_ROWS, ATTN_Q_DIM), BF16),
        pltpu.VMEM((MIX_ROWS, D_MODEL), F32),
        pltpu.VMEM((MIX_ROWS, D_MODEL), BF16),
        pltpu.VMEM((MIX_ROWS, D_MODEL), F32),
        pltpu.VMEM((SSM_SLABS, SSM_STATE, MXU_WIDTH), F32),
        pltpu.VMEM((2, ATTN_Q_HEADS, ATTN_BLOCK, 2 * ATTN_BLOCK), F32),
    ]
    return pl.pallas_call(
        functools.partial(_mixer_kernel, seq // MIX_ROWS),
        grid=(n_tiles + 1,),
        in_specs=in_specs,
        out_specs=finished(D_MODEL),
        out_shape=jax.ShapeDtypeStruct(h2d.shape, F32),
        scratch_shapes=scratch,
        compiler_params=pltpu.CompilerParams(dimension_semantics=("arbitrary",),
                                             vmem_limit_bytes=VMEM_LIMIT),
        name="mixer",
    )(sinks, table, h2d, gates, z, xbc, qkv, dt, a_log, d_exp, norm_g, bucket,
      w_ssm, w_attn, w_out, post_g)


def _reorder_w_in(w_in):
    o_z = 2 * D_MODEL
    o_dt = o_z + SSM_D_INNER + SSM_CONV_DIM
    o_q = o_dt + SSM_HEADS
    o_k = o_q + ATTN_Q_DIM
    o_v = o_k + ATTN_KV_DIM

    def dup_heads(w):
        w = w.reshape(D_MODEL, ATTN_KV_HEADS, 1, ATTN_HEAD_DIM)
        return jnp.broadcast_to(w, (D_MODEL, ATTN_KV_HEADS, 2, ATTN_HEAD_DIM)).reshape(D_MODEL, KV_DUP_DIM)

    w_q = w_in[:, o_q:o_k] * (ATTN_HEAD_DIM ** -0.5 * LOG2E)
    dt_cols = jnp.pad(w_in[:, o_dt:o_q], ((0, 0), (0, DT_PAD - SSM_HEADS)))
    return jnp.concatenate([w_in[:, :o_dt], w_q, dup_heads(w_in[:, o_k:o_v]), dup_heads(w_in[:, o_v:]), dt_cols],
                           axis=1)


def _pad_heads(v):
    return jnp.pad(v.astype(F32), (0, DT_PAD - SSM_HEADS))[None, :]


def kernel(x, ffn1_pre_g, ffn1_w_gate, ffn1_w_up, ffn1_w_down, ffn1_post_g, mix_pre_g, w_in, conv_w, conv_b, dt_bias, a_log, d_skip, ssm_norm_g, w_ssm_proj, attn_sinks, rel_bias_table, w_attn_proj, w_out, mix_post_g, ffn2_pre_g, ffn2_w_gate, ffn2_w_up, ffn2_w_down, ffn2_post_g):
    batch, seq, _ = x.shape
    depth = w_in.shape[0]
    h = x.reshape(batch * seq, D_MODEL)
    table = rel_bias_table.astype(F32) * LOG2E
    for l in range(depth):
        h = _ffn(h, ffn1_pre_g[l][None, :], ffn1_w_gate[l].astype(BF16), ffn1_w_up[l].astype(BF16),
                 ffn1_w_down[l].astype(BF16), ffn1_post_g[l][None, :])
        gates, z, xbc, qkv, dt = _inproj(h, mix_pre_g[l][None, :], _reorder_w_in(w_in[l]).astype(BF16),
                                         conv_w[l].astype(F32), conv_b[l].astype(F32)[None, :],
                                         _pad_heads(dt_bias[l]), seq)
        h = _mixer(h, gates, z, xbc, qkv, dt, seq, _pad_heads(a_log[l]),
                   jnp.repeat(d_skip[l].astype(F32), SSM_HEAD_DIM)[None, :], ssm_norm_g[l][None, :],
                   w_ssm_proj[l].astype(BF16), w_attn_proj[l].astype(BF16), w_out[l].astype(BF16),
                   mix_post_g[l][None, :], attn_sinks[l].astype(F32) * LOG2E, table)
        h = _ffn(h, ffn2_pre_g[l][None, :], ffn2_w_gate[l].astype(BF16), ffn2_w_up[l].astype(BF16),
                 ffn2_w_down[l].astype(BF16), ffn2_post_g[l][None, :])
    return h.reshape(batch, seq, D_MODEL)
```

```python
import functools
import math

import numpy as np
import jax
import jax.numpy as jnp
from jax import lax
from jax.experimental import pallas as pl
from jax.experimental.pallas import tpu as pltpu

F32 = jnp.float32
BF16 = jnp.bfloat16

D_MODEL = 1024
D_FF = 2816
FFN_RESIDUAL_WEIGHT = 0.5
SSM_D_INNER = 2 * D_MODEL
SSM_HEAD_DIM = 64
SSM_HEADS = SSM_D_INNER // SSM_HEAD_DIM
SSM_GROUPS = 4
SSM_STATE = 128
SSM_CONV = 4
SSM_CHUNK = 128
SSM_BC_DIM = SSM_GROUPS * SSM_STATE
SSM_CONV_DIM = SSM_D_INNER + 2 * SSM_BC_DIM
SSM_GROUP_WIDTH = SSM_D_INNER // SSM_GROUPS
ATTN_Q_HEADS = 16
ATTN_KV_HEADS = 4
ATTN_HEAD_DIM = 64
ATTN_WINDOW = 128
ATTN_BLOCK = 128
ATTN_Q_DIM = ATTN_Q_HEADS * ATTN_HEAD_DIM
ATTN_KV_DIM = ATTN_KV_HEADS * ATTN_HEAD_DIM
REL_BUCKETS = 32
REL_MAX_DISTANCE = 128
RMS_EPS = 1e-6
LOG2E = math.log2(math.e)

LANES = 128
SUBLANES = 8
MXU_WIDTH = 256
DT_PAD = LANES
CONV_TAIL = SUBLANES
CONV_SLAB = MXU_WIDTH
CONV_BUFFERS = 4

SLAB_HEADS = MXU_WIDTH // SSM_HEAD_DIM
SSM_SLABS = SSM_D_INNER // MXU_WIDTH
SLABS_PER_GROUP = SSM_SLABS // SSM_GROUPS
ATTN_PAIRS = ATTN_Q_DIM // LANES
PAIRS_PER_KV = ATTN_PAIRS // ATTN_KV_HEADS
KV_DUP_DIM = ATTN_KV_HEADS * LANES

COL_GATES = 0
COL_Z = COL_GATES + 2 * D_MODEL
COL_XBC = COL_Z + SSM_D_INNER
COL_QKV = COL_XBC + SSM_CONV_DIM
QKV_DIM = ATTN_Q_DIM + 2 * KV_DUP_DIM
COL_DT = COL_QKV + QKV_DIM
IN_COLS_PADDED = COL_DT + DT_PAD
ACT_DIM = SSM_D_INNER + SSM_CONV_DIM + QKV_DIM

FFN_ROWS = 512
MIX_ROWS = 256
VMEM_LIMIT = 56 * 1024 * 1024


def _rms(x, g):
    ms = jnp.mean(x * x, axis=-1, keepdims=True)
    return x * lax.rsqrt(ms + RMS_EPS) * g


def _sigmoid(x):
    return 0.5 * jnp.tanh(0.5 * x) + 0.5


def _silu(x):
    return x * _sigmoid(x)


def _softplus(x):
    return jnp.maximum(x, 0.0) + jnp.log1p(jnp.exp(-jnp.abs(x)))


def _dot(a, b):
    return jnp.dot(a, b, preferred_element_type=F32)


def _dot_nt(a, b):
    return lax.dot_general(a, b, (((1,), (1,)), ((), ())), preferred_element_type=F32)


def _split3(x):
    hi = x.astype(BF16)
    r1 = x - hi.astype(F32)
    mid = r1.astype(BF16)
    lo = (r1 - mid.astype(F32)).astype(BF16)
    return hi, mid, lo


def _split_act(act_ref):
    return (act_ref.at[:, 0:SSM_D_INNER], act_ref.at[:, SSM_D_INNER:SSM_D_INNER + SSM_CONV_DIM],
            act_ref.at[:, SSM_D_INNER + SSM_CONV_DIM:ACT_DIM])


def _resident(shape):
    return pl.BlockSpec(shape, lambda *_: (0,) * len(shape), pipeline_mode=pl.Buffered(1))


def _ffn_kernel(x_ref, pre_ref, wg_ref, wu_ref, wd_ref, post_ref, o_ref):
    half = x_ref.shape[0] // 2
    halves = (slice(0, half), slice(half, 2 * half))
    hmid = []
    for rows in halves:
        u = _rms(x_ref[rows, :], pre_ref[...]).astype(BF16)
        g = _dot(u, wg_ref[...])
        up = _dot(u, wu_ref[...])
        hmid.append((_silu(g) * up).astype(BF16))
    for rows, hm in zip(halves, hmid):
        f = _dot(hm, wd_ref[...])
        o_ref[rows, :] = x_ref[rows, :] + FFN_RESIDUAL_WEIGHT * _rms(f, post_ref[...])


def _ffn(x2d, pre_g, wg, wu, wd, post_g):
    t = x2d.shape[0]
    row = pl.BlockSpec((FFN_ROWS, D_MODEL), lambda i: (i, 0))
    return pl.pallas_call(
        _ffn_kernel,
        grid=(t // FFN_ROWS,),
        in_specs=[row, _resident((1, D_MODEL)), _resident((D_MODEL, D_FF)), _resident((D_MODEL, D_FF)),
                  _resident((D_FF, D_MODEL)), _resident((1, D_MODEL))],
        out_specs=row,
        out_shape=jax.ShapeDtypeStruct((t, D_MODEL), F32),
        compiler_params=pltpu.CompilerParams(dimension_semantics=("arbitrary",),
                                             vmem_limit_bytes=VMEM_LIMIT),
        name="ffn",
    )(x2d, pre_g, wg, wu, wd, post_g)


def _inproj_kernel(seq, h_ref, g_ref, w_ref, convw_ref, convb_ref, dtb_ref,
                   gates_o, act_o, dt_o, xpad, tail):
    rows = h_ref.shape[0]
    i = pl.program_id(0)
    z_o, xbc_o, qkv_o = _split_act(act_o)

    @pl.when((i * rows) % seq == 0)
    def _new_sequence():
        tail[...] = jnp.zeros_like(tail)

    u = _rms(h_ref[...], g_ref[...]).astype(BF16)

    def conv_slab(n):
        cols = slice(n * CONV_SLAB, (n + 1) * CONV_SLAB)
        buf = xpad.at[n % CONV_BUFFERS]
        buf[0:CONV_TAIL, :] = tail[:, cols]
        buf[CONV_TAIL:CONV_TAIL + rows, :] = _dot(u, w_ref[:, COL_XBC + n * CONV_SLAB:COL_XBC + (n + 1) * CONV_SLAB])
        conv = convb_ref[:, cols]
        for k in range(SSM_CONV):
            start = CONV_TAIL - (SSM_CONV - 1) + k
            conv = conv + convw_ref[k:k + 1, cols] * buf[start:start + rows, :]
        xbc_o[:, cols] = _silu(conv).astype(BF16)
        tail[:, cols] = buf[rows:rows + CONV_TAIL, :]

    def piece(out_ref, col0, n, act):
        cols = slice(n * CONV_SLAB, (n + 1) * CONV_SLAB)
        out_ref[:, cols] = act(_dot(u, w_ref[:, col0 + n * CONV_SLAB:col0 + (n + 1) * CONV_SLAB])).astype(BF16)

    def dt_piece():
        dt_o[...] = _softplus(_dot(u, w_ref[:, COL_DT:IN_COLS_PADDED]) + dtb_ref[...])

    others = []
    for n in range(QKV_DIM // CONV_SLAB):
        others.append(functools.partial(piece, gates_o, COL_GATES, n, _sigmoid))
        others.append(functools.partial(piece, qkv_o, COL_QKV, n, lambda v: v))
        others.append(functools.partial(piece, z_o, COL_Z, n, _silu))
    others.append(dt_piece)
    n_conv = SSM_CONV_DIM // CONV_SLAB
    for n in range(n_conv):
        conv_slab(n)
        for k in range(n * len(others) // n_conv, (n + 1) * len(others) // n_conv):
            others[k]()


def _inproj(h2d, g, w, conv_w, conv_b, dt_bias, seq):
    t = h2d.shape[0]

    def row(width):
        return pl.BlockSpec((FFN_ROWS, width), lambda i: (i, 0))

    widths = (2 * D_MODEL, ACT_DIM, DT_PAD)
    dtypes = (BF16, BF16, F32)
    return pl.pallas_call(
        functools.partial(_inproj_kernel, seq),
        grid=(t // FFN_ROWS,),
        in_specs=[row(D_MODEL), _resident((1, D_MODEL)), _resident((D_MODEL, IN_COLS_PADDED)),
                  _resident((SSM_CONV, SSM_CONV_DIM)), _resident((1, SSM_CONV_DIM)), _resident((1, DT_PAD))],
        out_specs=[row(w_) for w_ in widths],
        out_shape=[jax.ShapeDtypeStruct((t, w_), d_) for w_, d_ in zip(widths, dtypes)],
        scratch_shapes=[pltpu.VMEM((CONV_BUFFERS, CONV_TAIL + FFN_ROWS, CONV_SLAB), F32),
                        pltpu.VMEM((CONV_TAIL, SSM_CONV_DIM), F32)],
        compiler_params=pltpu.CompilerParams(dimension_semantics=("arbitrary",),
                                             vmem_limit_bytes=VMEM_LIMIT),
        name="inproj",
    )(h2d, g, w, conv_w, conv_b, dt_bias)


def _mixer_kernel(tiles_per_seq, sinks_ref, table_ref,
                  h_ref, gates_ref, act_ref, dt_ref,
                  alog_ref, dexp_ref, ng_ref, bucket_ref,
                  wssm_ref, wattn_ref, wout_ref, postg_ref,
                  o_ref,
                  kvpad, yscr, ynorm, attn, pssm, pmerged, pmix, state, bias):
    z_ref, xbc_ref, qkv_ref = _split_act(act_ref)
    rows_per_step = act_ref.shape[0]
    n_chunks = rows_per_step // SSM_CHUNK
    s = pl.program_id(0)
    q = SSM_CHUNK
    cur_base = (s % 2) * rows_per_step
    prev_rows = pl.ds(pl.multiple_of(rows_per_step - cur_base, rows_per_step), rows_per_step)

    def cur_rows(c):
        return pl.ds(pl.multiple_of(cur_base + c * q, q), q)


    ii = lax.broadcasted_iota(jnp.int32, (q, q), 0)
    jj = lax.broadcasted_iota(jnp.int32, (q, q), 1)
    causal = ii >= jj
    tril3 = jnp.concatenate([jnp.where(causal, 1.0, 0.0).astype(BF16)] * 3, axis=1)
    low_half = jj < SSM_HEAD_DIM
    low_half_row = low_half[0:1, :]
    slab_lane_head = lax.broadcasted_iota(jnp.int32, (q, MXU_WIDTH), 1) // SSM_HEAD_DIM
    band_low_half = lax.broadcasted_iota(jnp.int32, (2 * ATTN_BLOCK, LANES), 1) < ATTN_HEAD_DIM

    @pl.when(s == 0)
    def _init():
        bi = lax.broadcasted_iota(jnp.int32, (ATTN_BLOCK, 2 * ATTN_BLOCK), 0)
        bj = lax.broadcasted_iota(jnp.int32, (ATTN_BLOCK, 2 * ATTN_BLOCK), 1)
        dist = bi + ATTN_BLOCK - bj
        in_window = (dist >= 0) & (dist < ATTN_WINDOW)
        bucket = bucket_ref[...]
        for hq in range(ATTN_Q_HEADS):
            acc = jnp.zeros((ATTN_BLOCK, 2 * ATTN_BLOCK), F32)
            for bk in range(REL_BUCKETS):
                acc = jnp.where(bucket == bk, table_ref[bk, hq], acc)
            masked = jnp.where(in_window, acc, -jnp.inf)
            bias[0, hq] = masked
            bias[1, hq] = jnp.where(bj >= ATTN_BLOCK, masked, -jnp.inf)
        ynorm[rows_per_step:2 * rows_per_step, :] = jnp.zeros((rows_per_step, SSM_D_INNER), BF16)
        attn[rows_per_step:2 * rows_per_step, :] = jnp.zeros((rows_per_step, ATTN_Q_DIM), BF16)

    new_sequence = s % tiles_per_seq == 0

    @pl.when(new_sequence)
    def _reset_sequence_state():
        state[...] = jnp.zeros_like(state)
        kvpad[0:ATTN_BLOCK, :] = jnp.zeros((ATTN_BLOCK, 2 * KV_DUP_DIM), BF16)

    kvpad[ATTN_BLOCK:ATTN_BLOCK + rows_per_step, :] = qkv_ref[:, ATTN_Q_DIM:QKV_DIM]

    a_neg = -jnp.exp(alog_ref[...])

    pending = []
    for n in range(D_MODEL // MXU_WIDTH):
        ns = slice(n * MXU_WIDTH, (n + 1) * MXU_WIDTH)
        ns_attn = slice(D_MODEL + n * MXU_WIDTH, D_MODEL + (n + 1) * MXU_WIDTH)

        def ssm_item(ns=ns):
            pssm[:, ns] = _dot(ynorm[prev_rows, :], wssm_ref[:, ns])

        def attn_merge_item(ns=ns, ns_attn=ns_attn):
            y_attn = _dot(attn[prev_rows, :], wattn_ref[:, ns])
            merged = gates_ref[:, ns].astype(F32) * pssm[:, ns] + gates_ref[:, ns_attn].astype(F32) * y_attn
            pmerged[:, ns] = merged.astype(BF16)

        pending += [ssm_item, attn_merge_item]
    for n in range(D_MODEL // MXU_WIDTH):
        ns = slice(n * MXU_WIDTH, (n + 1) * MXU_WIDTH)

        def out_item(ns=ns):
            pmix[:, ns] = _dot(pmerged[...], wout_ref[:, ns])

        pending.append(out_item)

    def residual_item():
        o_ref[...] = h_ref[...] + _rms(pmix[...], postg_ref[...])

    pending.append(residual_item)
    n_slots = n_chunks * 2 * SSM_SLABS
    assert len(pending) <= n_slots
    issue_at = {(k * n_slots) // len(pending) for k in range(len(pending))}
    slot = [0]

    def issue_pending():
        if slot[0] in issue_at:
            pending.pop(0)()
        slot[0] += 1

    for c in range(n_chunks):
        rows = slice(c * q, (c + 1) * q)

        dt = dt_ref[rows, :]
        hi, mid, lo = _split3(dt * a_neg)
        acs2 = _dot(tril3, jnp.concatenate([hi, mid, lo], axis=0)) * LOG2E
        row2_t = (acs2 - jnp.log2(dt)).T

        first = jnp.where(new_sequence, 1, 0) if c == 0 else 0
        band = slice(c * q, c * q + 2 * ATTN_BLOCK)

        def attention_logits(pair):
            kv = pair // PAIRS_PER_KV
            kd = kvpad[band, kv * LANES:(kv + 1) * LANES]
            zero = jnp.zeros_like(kd)
            k_cat = jnp.concatenate([jnp.where(band_low_half, kd, zero), jnp.where(band_low_half, zero, kd)], axis=0)
            q_pair = qkv_ref[rows, pair * LANES:(pair + 1) * LANES]
            return _dot_nt(q_pair, k_cat)

        def attention_output(pair, logits):
            kv = pair // PAIRS_PER_KV
            vd = kvpad[band, KV_DUP_DIM + kv * LANES:KV_DUP_DIM + (kv + 1) * LANES]
            zero = jnp.zeros_like(vd)
            v_cat = jnp.concatenate([jnp.where(band_low_half, vd, zero), jnp.where(band_low_half, zero, vd)], axis=0)
            ps, invs = [], []
            for half in range(2):
                hq = 2 * pair + half
                l2 = logits[:, half * 2 * ATTN_BLOCK:(half + 1) * 2 * ATTN_BLOCK] + bias[first, hq]
                sink = sinks_ref[hq]
                mx = jnp.maximum(jnp.max(l2, axis=-1, keepdims=True), sink)
                p = jnp.exp2(l2 - mx)
                den = jnp.sum(p, axis=-1, keepdims=True) + jnp.exp2(sink - mx)
                ps.append(p.astype(BF16))
                invs.append(1.0 / den)
            o = _dot(jnp.concatenate(ps, axis=1), v_cat)
            o = o * jnp.where(low_half, invs[0], invs[1])
            attn[cur_rows(c), pair * LANES:(pair + 1) * LANES] = o.astype(BF16)

        logits_next = attention_logits(0)
        for m in range(SSM_SLABS):
            logits_m = logits_next
            issue_pending()
            g = m // SLABS_PER_GROUP
            if m % SLABS_PER_GROUP == 0:
                b_g = xbc_ref[rows, SSM_D_INNER + g * SSM_STATE:SSM_D_INNER + (g + 1) * SSM_STATE]
                c_g = xbc_ref[rows, SSM_D_INNER + SSM_BC_DIM + g * SSM_STATE:
                              SSM_D_INNER + SSM_BC_DIM + (g + 1) * SSM_STATE]
                scores = _dot_nt(c_g, b_g)
                b_gt = b_g.astype(F32).T
            y_off = _dot(c_g, state[m].astype(BF16))
            if m + 1 < ATTN_PAIRS:
                logits_next = attention_logits(m + 1)
            tops, bots, y_off_scaled, chunk_decay = [], [], [], []
            for r0 in range(0, SLAB_HEADS, 2):
                ebs, cds = [], []
                for r in (r0, r0 + 1):
                    h = m * SLAB_HEADS + r
                    colb = jnp.broadcast_to(acs2[:, h:h + 1], (q, q))
                    row2 = row2_t[h:h + 1, :]
                    lastb = colb[q - 1:q, :]
                    l_mat = jnp.exp2(jnp.where(causal, colb - row2, -jnp.inf)) * scores
                    tops.append(l_mat.astype(BF16))
                    bots.append((b_gt * jnp.exp2(lastb - row2)).astype(BF16))
                    ebs.append(jnp.exp2(colb))
                    cds.append(jnp.exp2(lastb))
                lanes = slice(r0 * SSM_HEAD_DIM, (r0 + 2) * SSM_HEAD_DIM)
                y_off_scaled.append(y_off[:, lanes] * jnp.where(low_half, ebs[0], ebs[1]))
                chunk_decay.append(jnp.where(low_half_row, cds[0], cds[1]))
            lhs = jnp.concatenate([jnp.concatenate(tops, axis=1), jnp.concatenate(bots, axis=1)], axis=0)
            xs_slab = xbc_ref[rows, m * MXU_WIDTH:(m + 1) * MXU_WIDTH]
            rhs = jnp.concatenate(
                [jnp.where(slab_lane_head == r, xs_slab, jnp.zeros_like(xs_slab)) for r in range(SLAB_HEADS)],
                axis=0)
            yy = _dot(lhs, rhs)
            yscr[:, m * MXU_WIDTH:(m + 1) * MXU_WIDTH] = yy[0:q] + jnp.concatenate(y_off_scaled, axis=1)
            state[m] = state[m] * jnp.concatenate(chunk_decay, axis=1) + yy[q:2 * q]
            issue_pending()
            attention_output(m, logits_m)

        xs = xbc_ref[rows, 0:SSM_D_INNER].astype(F32)
        y = yscr[...] + dexp_ref[...] * xs
        yg = y * z_ref[rows, :].astype(F32)
        for g in range(SSM_GROUPS):
            gl = slice(g * SSM_GROUP_WIDTH, (g + 1) * SSM_GROUP_WIDTH)
            v = yg[:, gl]
            ms = jnp.mean(v * v, axis=-1, keepdims=True)
            ynorm[cur_rows(c), gl] = (v * lax.rsqrt(ms + RMS_EPS) * ng_ref[:, gl]).astype(BF16)

    assert not pending

    kvpad[0:ATTN_BLOCK, :] = kvpad[rows_per_step:rows_per_step + ATTN_BLOCK, :]


def _t5_bucket_matrix():
    qi = np.arange(ATTN_BLOCK)[:, None]
    kj = np.arange(2 * ATTN_BLOCK)[None, :]
    dist = np.maximum(qi + ATTN_BLOCK - kj, 0).astype(np.int32)
    max_exact = REL_BUCKETS // 2
    d = np.maximum(dist, 1).astype(np.float32)
    large = max_exact + (np.log(d / np.float32(max_exact)) / np.float32(math.log(REL_MAX_DISTANCE / max_exact))
                         * np.float32(REL_BUCKETS - max_exact)).astype(np.int32)
    large = np.minimum(large, REL_BUCKETS - 1)
    return np.where(dist < max_exact, dist, large).astype(np.int32)


def _mixer(h2d, gates, act, dt, seq, a_log, d_exp, norm_g, w_ssm, w_attn, w_out, post_g, sinks, table):
    n_tiles = h2d.shape[0] // MIX_ROWS

    def scanned(width):
        return pl.BlockSpec((MIX_ROWS, width), lambda s: (jnp.minimum(s, n_tiles - 1), 0))

    def finished(width):
        return pl.BlockSpec((MIX_ROWS, width), lambda s: (jnp.maximum(s - 1, 0), 0))

    smem = pl.BlockSpec(memory_space=pltpu.SMEM)
    bucket = jnp.asarray(_t5_bucket_matrix())
    in_specs = [
        smem, smem,
        finished(D_MODEL), finished(2 * D_MODEL),
        scanned(ACT_DIM), scanned(DT_PAD),
        _resident((1, DT_PAD)), _resident((1, SSM_D_INNER)), _resident((1, SSM_D_INNER)),
        _resident((ATTN_BLOCK, 2 * ATTN_BLOCK)),
        _resident((SSM_D_INNER, D_MODEL)), _resident((ATTN_Q_DIM, D_MODEL)), _resident((D_MODEL, D_MODEL)),
        _resident((1, D_MODEL)),
    ]
    scratch = [
        pltpu.VMEM((ATTN_BLOCK + MIX_ROWS, 2 * KV_DUP_DIM), BF16),
        pltpu.VMEM((SSM_CHUNK, SSM_D_INNER), F32),
        pltpu.VMEM((2 * MIX_ROWS, SSM_D_INNER), BF16),
        pltpu.VMEM((2 * MIX_ROWS, ATTN_Q_DIM), BF16),
        pltpu.VMEM((MIX_ROWS, D_MODEL), F32),
        pltpu.VMEM((MIX_ROWS, D_MODEL), BF16),
        pltpu.VMEM((MIX_ROWS, D_MODEL), F32),
        pltpu.VMEM((SSM_SLABS, SSM_STATE, MXU_WIDTH), F32),
        pltpu.VMEM((2, ATTN_Q_HEADS, ATTN_BLOCK, 2 * ATTN_BLOCK), F32),
    ]
    return pl.pallas_call(
        functools.partial(_mixer_kernel, seq // MIX_ROWS),
        grid=(n_tiles + 1,),
        in_specs=in_specs,
        out_specs=finished(D_MODEL),
        out_shape=jax.ShapeDtypeStruct(h2d.shape, F32),
        scratch_shapes=scratch,
        compiler_params=pltpu.CompilerParams(dimension_semantics=("arbitrary",),
                                             vmem_limit_bytes=VMEM_LIMIT),
        name="mixer",
    )(sinks, table, h2d, gates, act, dt, a_log, d_exp, norm_g, bucket,
      w_ssm, w_attn, w_out, post_g)


def _reorder_w_in(w_in):
    o_z = 2 * D_MODEL
    o_dt = o_z + SSM_D_INNER + SSM_CONV_DIM
    o_q = o_dt + SSM_HEADS
    o_k = o_q + ATTN_Q_DIM
    o_v = o_k + ATTN_KV_DIM

    def dup_heads(w):
        w = w.reshape(D_MODEL, ATTN_KV_HEADS, 1, ATTN_HEAD_DIM)
        return jnp.broadcast_to(w, (D_MODEL, ATTN_KV_HEADS, 2, ATTN_HEAD_DIM)).reshape(D_MODEL, KV_DUP_DIM)

    w_q = w_in[:, o_q:o_k] * (ATTN_HEAD_DIM ** -0.5 * LOG2E)
    dt_cols = jnp.pad(w_in[:, o_dt:o_q], ((0, 0), (0, DT_PAD - SSM_HEADS)))
    return jnp.concatenate([w_in[:, :o_dt], w_q, dup_heads(w_in[:, o_k:o_v]), dup_heads(w_in[:, o_v:]), dt_cols],
                           axis=1)


def _pad_heads(v):
    return jnp.pad(v.astype(F32), (0, DT_PAD - SSM_HEADS))[None, :]


def kernel(x, ffn1_pre_g, ffn1_w_gate, ffn1_w_up, ffn1_w_down, ffn1_post_g, mix_pre_g, w_in, conv_w, conv_b, dt_bias, a_log, d_skip, ssm_norm_g, w_ssm_proj, attn_sinks, rel_bias_table, w_attn_proj, w_out, mix_post_g, ffn2_pre_g, ffn2_w_gate, ffn2_w_up, ffn2_w_down, ffn2_post_g):
    batch, seq, _ = x.shape
    depth = w_in.shape[0]
    h = x.reshape(batch * seq, D_MODEL)
    table = rel_bias_table.astype(F32) * LOG2E
    for l in range(depth):
        h = _ffn(h, ffn1_pre_g[l][None, :], ffn1_w_gate[l].astype(BF16), ffn1_w_up[l].astype(BF16),
                 ffn1_w_down[l].astype(BF16), ffn1_post_g[l][None, :])
        gates, act, dt = _inproj(h, mix_pre_g[l][None, :], _reorder_w_in(w_in[l]).astype(BF16),
                                         conv_w[l].astype(F32), conv_b[l].astype(F32)[None, :],
                                         _pad_heads(dt_bias[l]), seq)
        h = _mixer(h, gates, act, dt, seq, _pad_heads(a_log[l]),
                   jnp.repeat(d_skip[l].astype(F32), SSM_HEAD_DIM)[None, :], ssm_norm_g[l][None, :],
                   w_ssm_proj[l].astype(BF16), w_attn_proj[l].astype(BF16), w_out[l].astype(BF16),
                   mix_post_g[l][None, :], attn_sinks[l].astype(F32) * LOG2E, table)
        h = _ffn(h, ffn2_pre_g[l][None, :], ffn2_w_gate[l].astype(BF16), ffn2_w_up[l].astype(BF16),
                 ffn2_w_down[l].astype(BF16), ffn2_post_g[l][None, :])
    return h.reshape(batch, seq, D_MODEL)
```

```python
import functools
import math

import numpy as np
import jax
import jax.numpy as jnp
from jax import lax
from jax.experimental import pallas as pl
from jax.experimental.pallas import tpu as pltpu

F32 = jnp.float32
BF16 = jnp.bfloat16

D_MODEL = 1024
D_FF = 2816
FFN_RESIDUAL_WEIGHT = 0.5
SSM_D_INNER = 2 * D_MODEL
SSM_HEAD_DIM = 64
SSM_HEADS = SSM_D_INNER // SSM_HEAD_DIM
SSM_GROUPS = 4
SSM_STATE = 128
SSM_CONV = 4
SSM_CHUNK = 128
SSM_BC_DIM = SSM_GROUPS * SSM_STATE
SSM_CONV_DIM = SSM_D_INNER + 2 * SSM_BC_DIM
SSM_GROUP_WIDTH = SSM_D_INNER // SSM_GROUPS
ATTN_Q_HEADS = 16
ATTN_KV_HEADS = 4
ATTN_HEAD_DIM = 64
ATTN_WINDOW = 128
ATTN_BLOCK = 128
ATTN_Q_DIM = ATTN_Q_HEADS * ATTN_HEAD_DIM
ATTN_KV_DIM = ATTN_KV_HEADS * ATTN_HEAD_DIM
REL_BUCKETS = 32
REL_MAX_DISTANCE = 128
RMS_EPS = 1e-6
LOG2E = math.log2(math.e)

LANES = 128
SUBLANES = 8
MXU_WIDTH = 256
DT_PAD = LANES
CONV_TAIL = SUBLANES
CONV_SLAB = MXU_WIDTH
CONV_BUFFERS = 4

SLAB_HEADS = MXU_WIDTH // SSM_HEAD_DIM
SSM_SLABS = SSM_D_INNER // MXU_WIDTH
SLABS_PER_GROUP = SSM_SLABS // SSM_GROUPS
ATTN_PAIRS = ATTN_Q_DIM // LANES
PAIRS_PER_KV = ATTN_PAIRS // ATTN_KV_HEADS
KV_DUP_DIM = ATTN_KV_HEADS * LANES

COL_GATES = 0
COL_Z = COL_GATES + 2 * D_MODEL
COL_XBC = COL_Z + SSM_D_INNER
COL_QKV = COL_XBC + SSM_CONV_DIM
QKV_DIM = ATTN_Q_DIM + 2 * KV_DUP_DIM
COL_DT = COL_QKV + QKV_DIM
IN_COLS_PADDED = COL_DT + DT_PAD
ACT_DIM = SSM_D_INNER + SSM_CONV_DIM + QKV_DIM

FFN_ROWS = 512
MIX_ROWS = 512
PROJ_ROWS = 256
VMEM_LIMIT = 56 * 1024 * 1024
MIXER_VMEM_LIMIT = 60 * 1024 * 1024


def _rms(x, g):
    ms = jnp.mean(x * x, axis=-1, keepdims=True)
    return x * lax.rsqrt(ms + RMS_EPS) * g


def _sigmoid(x):
    return 0.5 * jnp.tanh(0.5 * x) + 0.5


def _silu(x):
    return x * _sigmoid(x)


def _softplus(x):
    return jnp.maximum(x, 0.0) + jnp.log1p(jnp.exp(-jnp.abs(x)))


def _dot(a, b):
    return jnp.dot(a, b, preferred_element_type=F32)


def _dot_nt(a, b):
    return lax.dot_general(a, b, (((1,), (1,)), ((), ())), preferred_element_type=F32)


def _split3(x):
    hi = x.astype(BF16)
    r1 = x - hi.astype(F32)
    mid = r1.astype(BF16)
    lo = (r1 - mid.astype(F32)).astype(BF16)
    return hi, mid, lo


def _split_act(act_ref):
    return (act_ref.at[:, 0:SSM_D_INNER], act_ref.at[:, SSM_D_INNER:SSM_D_INNER + SSM_CONV_DIM],
            act_ref.at[:, SSM_D_INNER + SSM_CONV_DIM:ACT_DIM])


def _resident(shape):
    return pl.BlockSpec(shape, lambda *_: (0,) * len(shape), pipeline_mode=pl.Buffered(1))


def _ffn_kernel(x_ref, pre_ref, wg_ref, wu_ref, wd_ref, post_ref, o_ref):
    x = x_ref[...]
    u = _rms(x, pre_ref[...]).astype(BF16)
    g = _dot(u, wg_ref[...])
    up = _dot(u, wu_ref[...])
    hmid = (_silu(g) * up).astype(BF16)
    f = _dot(hmid, wd_ref[...])
    o_ref[...] = x + FFN_RESIDUAL_WEIGHT * _rms(f, post_ref[...])


def _ffn(x2d, pre_g, wg, wu, wd, post_g):
    t = x2d.shape[0]
    row = pl.BlockSpec((FFN_ROWS, D_MODEL), lambda i: (i, 0))
    return pl.pallas_call(
        _ffn_kernel,
        grid=(t // FFN_ROWS,),
        in_specs=[row, _resident((1, D_MODEL)), _resident((D_MODEL, D_FF)), _resident((D_MODEL, D_FF)),
                  _resident((D_FF, D_MODEL)), _resident((1, D_MODEL))],
        out_specs=row,
        out_shape=jax.ShapeDtypeStruct((t, D_MODEL), F32),
        compiler_params=pltpu.CompilerParams(dimension_semantics=("arbitrary",),
                                             vmem_limit_bytes=VMEM_LIMIT),
        name="ffn",
    )(x2d, pre_g, wg, wu, wd, post_g)


def _inproj_kernel(seq, h_ref, g_ref, w_ref, convw_ref, convb_ref, dtb_ref,
                   gates_o, act_o, dt_o, xpad, tail):
    rows = h_ref.shape[0]
    i = pl.program_id(0)
    z_o, xbc_o, qkv_o = _split_act(act_o)

    @pl.when((i * rows) % seq == 0)
    def _new_sequence():
        tail[...] = jnp.zeros_like(tail)

    u = _rms(h_ref[...], g_ref[...]).astype(BF16)

    def conv_slab(n):
        cols = slice(n * CONV_SLAB, (n + 1) * CONV_SLAB)
        buf = xpad.at[n % CONV_BUFFERS]
        buf[0:CONV_TAIL, :] = tail[:, cols]
        buf[CONV_TAIL:CONV_TAIL + rows, :] = _dot(u, w_ref[:, COL_XBC + n * CONV_SLAB:COL_XBC + (n + 1) * CONV_SLAB])
        conv = convb_ref[:, cols]
        for k in range(SSM_CONV):
            start = CONV_TAIL - (SSM_CONV - 1) + k
            conv = conv + convw_ref[k:k + 1, cols] * buf[start:start + rows, :]
        xbc_o[:, cols] = _silu(conv).astype(BF16)
        tail[:, cols] = buf[rows:rows + CONV_TAIL, :]

    def piece(out_ref, col0, n, act):
        cols = slice(n * CONV_SLAB, (n + 1) * CONV_SLAB)
        out_ref[:, cols] = act(_dot(u, w_ref[:, col0 + n * CONV_SLAB:col0 + (n + 1) * CONV_SLAB])).astype(BF16)

    def dt_piece():
        dt_o[...] = _softplus(_dot(u, w_ref[:, COL_DT:IN_COLS_PADDED]) + dtb_ref[...])

    others = []
    for n in range(QKV_DIM // CONV_SLAB):
        others.append(functools.partial(piece, gates_o, COL_GATES, n, _sigmoid))
        others.append(functools.partial(piece, qkv_o, COL_QKV, n, lambda v: v))
        others.append(functools.partial(piece, z_o, COL_Z, n, _silu))
    others.append(dt_piece)
    n_conv = SSM_CONV_DIM // CONV_SLAB
    for n in range(n_conv):
        conv_slab(n)
        for k in range(n * len(others) // n_conv, (n + 1) * len(others) // n_conv):
            others[k]()


def _inproj(h2d, g, w, conv_w, conv_b, dt_bias, seq):
    t = h2d.shape[0]

    def row(width):
        return pl.BlockSpec((FFN_ROWS, width), lambda i: (i, 0))

    widths = (2 * D_MODEL, ACT_DIM, DT_PAD)
    dtypes = (BF16, BF16, F32)
    return pl.pallas_call(
        functools.partial(_inproj_kernel, seq),
        grid=(t // FFN_ROWS,),
        in_specs=[row(D_MODEL), _resident((1, D_MODEL)), _resident((D_MODEL, IN_COLS_PADDED)),
                  _resident((SSM_CONV, SSM_CONV_DIM)), _resident((1, SSM_CONV_DIM)), _resident((1, DT_PAD))],
        out_specs=[row(w_) for w_ in widths],
        out_shape=[jax.ShapeDtypeStruct((t, w_), d_) for w_, d_ in zip(widths, dtypes)],
        scratch_shapes=[pltpu.VMEM((CONV_BUFFERS, CONV_TAIL + FFN_ROWS, CONV_SLAB), F32),
                        pltpu.VMEM((CONV_TAIL, SSM_CONV_DIM), F32)],
        compiler_params=pltpu.CompilerParams(dimension_semantics=("arbitrary",),
                                             vmem_limit_bytes=VMEM_LIMIT),
        name="inproj",
    )(h2d, g, w, conv_w, conv_b, dt_bias)


def _mixer_kernel(tiles_per_seq, sinks_ref, table_ref,
                  h_ref, gates_ref, act_ref, dt_ref,
                  alog_ref, dexp_ref, ng_ref, bucket_ref,
                  wssm_ref, wattn_ref, wout_ref, postg_ref,
                  o_ref,
                  kvpad, yscr, ynorm, attn, pssm, pmerged, pmix, state, bias):
    z_ref, xbc_ref, qkv_ref = _split_act(act_ref)
    rows_per_step = act_ref.shape[0]
    n_chunks = rows_per_step // SSM_CHUNK
    s = pl.program_id(0)
    q = SSM_CHUNK
    cur_base = (s % 2) * rows_per_step
    prev_base = rows_per_step - cur_base

    def cur_rows(c):
        return pl.ds(pl.multiple_of(cur_base + c * q, q), q)


    ii = lax.broadcasted_iota(jnp.int32, (q, q), 0)
    jj = lax.broadcasted_iota(jnp.int32, (q, q), 1)
    causal = ii >= jj
    tril3 = jnp.concatenate([jnp.where(causal, 1.0, 0.0).astype(BF16)] * 3, axis=1)
    low_half = jj < SSM_HEAD_DIM
    low_half_row = low_half[0:1, :]
    slab_lane_head = lax.broadcasted_iota(jnp.int32, (q, MXU_WIDTH), 1) // SSM_HEAD_DIM
    band_low_half = lax.broadcasted_iota(jnp.int32, (2 * ATTN_BLOCK, LANES), 1) < ATTN_HEAD_DIM

    @pl.when(s == 0)
    def _init():
        bi = lax.broadcasted_iota(jnp.int32, (ATTN_BLOCK, 2 * ATTN_BLOCK), 0)
        bj = lax.broadcasted_iota(jnp.int32, (ATTN_BLOCK, 2 * ATTN_BLOCK), 1)
        dist = bi + ATTN_BLOCK - bj
        in_window = (dist >= 0) & (dist < ATTN_WINDOW)
        bucket = bucket_ref[...]
        for hq in range(ATTN_Q_HEADS):
            acc = jnp.zeros((ATTN_BLOCK, 2 * ATTN_BLOCK), F32)
            for bk in range(REL_BUCKETS):
                acc = jnp.where(bucket == bk, table_ref[bk, hq], acc)
            bias[hq] = jnp.where(in_window, acc, -jnp.inf)
        ynorm[rows_per_step:2 * rows_per_step, :] = jnp.zeros((rows_per_step, SSM_D_INNER), BF16)
        attn[rows_per_step:2 * rows_per_step, :] = jnp.zeros((rows_per_step, ATTN_Q_DIM), BF16)

    new_sequence = s % tiles_per_seq == 0

    @pl.when(new_sequence)
    def _reset_sequence_state():
        state[...] = jnp.zeros_like(state)
        kvpad[0:ATTN_BLOCK, :] = jnp.zeros((ATTN_BLOCK, 2 * KV_DUP_DIM), BF16)

    kvpad[ATTN_BLOCK:ATTN_BLOCK + rows_per_step, :] = qkv_ref[:, ATTN_Q_DIM:QKV_DIM]

    a_neg = -jnp.exp(alog_ref[...])

    pending = []
    for blk in range(rows_per_step // PROJ_ROWS):
        brows = slice(blk * PROJ_ROWS, (blk + 1) * PROJ_ROWS)
        prows = pl.ds(pl.multiple_of(prev_base + blk * PROJ_ROWS, PROJ_ROWS), PROJ_ROWS)
        for n in range(D_MODEL // MXU_WIDTH):
            ns = slice(n * MXU_WIDTH, (n + 1) * MXU_WIDTH)
            ns_attn = slice(D_MODEL + n * MXU_WIDTH, D_MODEL + (n + 1) * MXU_WIDTH)

            def ssm_item(ns=ns, prows=prows):
                pssm[:, ns] = _dot(ynorm[prows, :], wssm_ref[:, ns])

            def attn_merge_item(ns=ns, ns_attn=ns_attn, prows=prows, brows=brows):
                y_attn = _dot(attn[prows, :], wattn_ref[:, ns])
                merged = (gates_ref[brows, ns].astype(F32) * pssm[:, ns]
                          + gates_ref[brows, ns_attn].astype(F32) * y_attn)
                pmerged[:, ns] = merged.astype(BF16)

            pending += [ssm_item, attn_merge_item]
        for n in range(D_MODEL // MXU_WIDTH):
            ns = slice(n * MXU_WIDTH, (n + 1) * MXU_WIDTH)

            def out_item(ns=ns):
                pmix[:, ns] = _dot(pmerged[...], wout_ref[:, ns])

            pending.append(out_item)

        def residual_item(brows=brows):
            o_ref[brows, :] = h_ref[brows, :] + _rms(pmix[...], postg_ref[...])

        pending.append(residual_item)
    n_slots = n_chunks * 2 * SSM_SLABS
    assert len(pending) <= n_slots
    issue_at = {(k * n_slots) // len(pending) for k in range(len(pending))}
    slot = [0]

    def issue_pending():
        if slot[0] in issue_at:
            pending.pop(0)()
        slot[0] += 1

    for c in range(n_chunks):
        rows = slice(c * q, (c + 1) * q)

        dt = dt_ref[rows, :]
        hi, mid, lo = _split3(dt * a_neg)
        acs2 = _dot(tril3, jnp.concatenate([hi, mid, lo], axis=0)) * LOG2E
        row2_t = (acs2 - jnp.log2(dt)).T

        band = slice(c * q, c * q + 2 * ATTN_BLOCK)
        if c == 0:
            key_idx = lax.broadcasted_iota(jnp.int32, (ATTN_BLOCK, 2 * ATTN_BLOCK), 1)
            no_prev_block = jnp.where(new_sequence & (key_idx < ATTN_BLOCK), -jnp.inf, 0.0)

        def attention_logits(pair):
            kv = pair // PAIRS_PER_KV
            kd = kvpad[band, kv * LANES:(kv + 1) * LANES]
            zero = jnp.zeros_like(kd)
            k_cat = jnp.concatenate([jnp.where(band_low_half, kd, zero), jnp.where(band_low_half, zero, kd)], axis=0)
            q_pair = qkv_ref[rows, pair * LANES:(pair + 1) * LANES]
            return _dot_nt(q_pair, k_cat)

        def attention_output(pair, logits):
            kv = pair // PAIRS_PER_KV
            vd = kvpad[band, KV_DUP_DIM + kv * LANES:KV_DUP_DIM + (kv + 1) * LANES]
            zero = jnp.zeros_like(vd)
            v_cat = jnp.concatenate([jnp.where(band_low_half, vd, zero), jnp.where(band_low_half, zero, vd)], axis=0)
            ps, invs = [], []
            for half in range(2):
                hq = 2 * pair + half
                l2 = logits[:, half * 2 * ATTN_BLOCK:(half + 1) * 2 * ATTN_BLOCK] + bias[hq]
                if c == 0:
                    l2 = l2 + no_prev_block
                sink = sinks_ref[hq]
                mx = jnp.maximum(jnp.max(l2, axis=-1, keepdims=True), sink)
                p = jnp.exp2(l2 - mx)
                den = jnp.sum(p, axis=-1, keepdims=True) + jnp.exp2(sink - mx)
                ps.append(p.astype(BF16))
                invs.append(1.0 / den)
            o = _dot(jnp.concatenate(ps, axis=1), v_cat)
            o = o * jnp.where(low_half, invs[0], invs[1])
            attn[cur_rows(c), pair * LANES:(pair + 1) * LANES] = o.astype(BF16)

        logits_next = attention_logits(0)
        for m in range(SSM_SLABS):
            logits_m = logits_next
            issue_pending()
            g = m // SLABS_PER_GROUP
            if m % SLABS_PER_GROUP == 0:
                b_g = xbc_ref[rows, SSM_D_INNER + g * SSM_STATE:SSM_D_INNER + (g + 1) * SSM_STATE]
                c_g = xbc_ref[rows, SSM_D_INNER + SSM_BC_DIM + g * SSM_STATE:
                              SSM_D_INNER + SSM_BC_DIM + (g + 1) * SSM_STATE]
                scores = _dot_nt(c_g, b_g)
                b_gt = b_g.astype(F32).T
            y_off = _dot(c_g, state[m].astype(BF16))
            if m + 1 < ATTN_PAIRS:
                logits_next = attention_logits(m + 1)
            tops, bots, y_off_scaled, chunk_decay = [], [], [], []
            for r0 in range(0, SLAB_HEADS, 2):
                ebs, cds = [], []
                for r in (r0, r0 + 1):
                    h = m * SLAB_HEADS + r
                    colb = jnp.broadcast_to(acs2[:, h:h + 1], (q, q))
                    row2 = row2_t[h:h + 1, :]
                    lastb = colb[q - 1:q, :]
                    l_mat = jnp.exp2(jnp.where(causal, colb - row2, -jnp.inf)) * scores
                    tops.append(l_mat.astype(BF16))
                    bots.append((b_gt * jnp.exp2(lastb - row2)).astype(BF16))
                    ebs.append(jnp.exp2(colb))
                    cds.append(jnp.exp2(lastb))
                lanes = slice(r0 * SSM_HEAD_DIM, (r0 + 2) * SSM_HEAD_DIM)
                y_off_scaled.append(y_off[:, lanes] * jnp.where(low_half, ebs[0], ebs[1]))
                chunk_decay.append(jnp.where(low_half_row, cds[0], cds[1]))
            lhs = jnp.concatenate([jnp.concatenate(tops, axis=1), jnp.concatenate(bots, axis=1)], axis=0)
            xs_slab = xbc_ref[rows, m * MXU_WIDTH:(m + 1) * MXU_WIDTH]
            rhs = jnp.concatenate(
                [jnp.where(slab_lane_head == r, xs_slab, jnp.zeros_like(xs_slab)) for r in range(SLAB_HEADS)],
                axis=0)
            yy = _dot(lhs, rhs)
            yscr[:, m * MXU_WIDTH:(m + 1) * MXU_WIDTH] = yy[0:q] + jnp.concatenate(y_off_scaled, axis=1)
            state[m] = state[m] * jnp.concatenate(chunk_decay, axis=1) + yy[q:2 * q]
            issue_pending()
            attention_output(m, logits_m)

        xs = xbc_ref[rows, 0:SSM_D_INNER].astype(F32)
        y = yscr[...] + dexp_ref[...] * xs
        yg = y * z_ref[rows, :].astype(F32)
        for g in range(SSM_GROUPS):
            gl = slice(g * SSM_GROUP_WIDTH, (g + 1) * SSM_GROUP_WIDTH)
            v = yg[:, gl]
            ms = jnp.mean(v * v, axis=-1, keepdims=True)
            ynorm[cur_rows(c), gl] = (v * lax.rsqrt(ms + RMS_EPS) * ng_ref[:, gl]).astype(BF16)

    assert not pending

    kvpad[0:ATTN_BLOCK, :] = kvpad[rows_per_step:rows_per_step + ATTN_BLOCK, :]


def _t5_bucket_matrix():
    qi = np.arange(ATTN_BLOCK)[:, None]
    kj = np.arange(2 * ATTN_BLOCK)[None, :]
    dist = np.maximum(qi + ATTN_BLOCK - kj, 0).astype(np.int32)
    max_exact = REL_BUCKETS // 2
    d = np.maximum(dist, 1).astype(np.float32)
    large = max_exact + (np.log(d / np.float32(max_exact)) / np.float32(math.log(REL_MAX_DISTANCE / max_exact))
                         * np.float32(REL_BUCKETS - max_exact)).astype(np.int32)
    large = np.minimum(large, REL_BUCKETS - 1)
    return np.where(dist < max_exact, dist, large).astype(np.int32)


def _mixer(h2d, gates, act, dt, seq, a_log, d_exp, norm_g, w_ssm, w_attn, w_out, post_g, sinks, table):
    n_tiles = h2d.shape[0] // MIX_ROWS

    def scanned(width):
        return pl.BlockSpec((MIX_ROWS, width), lambda s: (jnp.minimum(s, n_tiles - 1), 0))

    def finished(width):
        return pl.BlockSpec((MIX_ROWS, width), lambda s: (jnp.maximum(s - 1, 0), 0))

    smem = pl.BlockSpec(memory_space=pltpu.SMEM)
    bucket = jnp.asarray(_t5_bucket_matrix())
    in_specs = [
        smem, smem,
        finished(D_MODEL), finished(2 * D_MODEL),
        scanned(ACT_DIM), scanned(DT_PAD),
        _resident((1, DT_PAD)), _resident((1, SSM_D_INNER)), _resident((1, SSM_D_INNER)),
        _resident((ATTN_BLOCK, 2 * ATTN_BLOCK)),
        _resident((SSM_D_INNER, D_MODEL)), _resident((ATTN_Q_DIM, D_MODEL)), _resident((D_MODEL, D_MODEL)),
        _resident((1, D_MODEL)),
    ]
    scratch = [
        pltpu.VMEM((ATTN_BLOCK + MIX_ROWS, 2 * KV_DUP_DIM), BF16),
        pltpu.VMEM((SSM_CHUNK, SSM_D_INNER), F32),
        pltpu.VMEM((2 * MIX_ROWS, SSM_D_INNER), BF16),
        pltpu.VMEM((2 * MIX_ROWS, ATTN_Q_DIM), BF16),
        pltpu.VMEM((PROJ_ROWS, D_MODEL), F32),
        pltpu.VMEM((PROJ_ROWS, D_MODEL), BF16),
        pltpu.VMEM((PROJ_ROWS, D_MODEL), F32),
        pltpu.VMEM((SSM_SLABS, SSM_STATE, MXU_WIDTH), F32),
        pltpu.VMEM((ATTN_Q_HEADS, ATTN_BLOCK, 2 * ATTN_BLOCK), F32),
    ]
    return pl.pallas_call(
        functools.partial(_mixer_kernel, seq // MIX_ROWS),
        grid=(n_tiles + 1,),
        in_specs=in_specs,
        out_specs=finished(D_MODEL),
        out_shape=jax.ShapeDtypeStruct(h2d.shape, F32),
        scratch_shapes=scratch,
        compiler_params=pltpu.CompilerParams(dimension_semantics=("arbitrary",),
                                             vmem_limit_bytes=MIXER_VMEM_LIMIT),
        name="mixer",
    )(sinks, table, h2d, gates, act, dt, a_log, d_exp, norm_g, bucket,
      w_ssm, w_attn, w_out, post_g)


def _reorder_w_in(w_in):
    o_z = 2 * D_MODEL
    o_dt = o_z + SSM_D_INNER + SSM_CONV_DIM
    o_q = o_dt + SSM_HEADS
    o_k = o_q + ATTN_Q_DIM
    o_v = o_k + ATTN_KV_DIM

    def dup_heads(w):
        w = w.reshape(D_MODEL, ATTN_KV_HEADS, 1, ATTN_HEAD_DIM)
        return jnp.broadcast_to(w, (D_MODEL, ATTN_KV_HEADS, 2, ATTN_HEAD_DIM)).reshape(D_MODEL, KV_DUP_DIM)

    w_q = w_in[:, o_q:o_k] * (ATTN_HEAD_DIM ** -0.5 * LOG2E)
    dt_cols = jnp.pad(w_in[:, o_dt:o_q], ((0, 0), (0, DT_PAD - SSM_HEADS)))
    parts = [w_in[:, :o_dt], w_q, dup_heads(w_in[:, o_k:o_v]), dup_heads(w_in[:, o_v:]), dt_cols]
    return jnp.concatenate([p.astype(BF16) for p in parts], axis=1)


def _pad_heads(v):
    return jnp.pad(v.astype(F32), (0, DT_PAD - SSM_HEADS))[None, :]


def kernel(x, ffn1_pre_g, ffn1_w_gate, ffn1_w_up, ffn1_w_down, ffn1_post_g, mix_pre_g, w_in, conv_w, conv_b, dt_bias, a_log, d_skip, ssm_norm_g, w_ssm_proj, attn_sinks, rel_bias_table, w_attn_proj, w_out, mix_post_g, ffn2_pre_g, ffn2_w_gate, ffn2_w_up, ffn2_w_down, ffn2_post_g):
    batch, seq, _ = x.shape
    depth = w_in.shape[0]
    h = x.reshape(batch * seq, D_MODEL)
    table = rel_bias_table.astype(F32) * LOG2E
    for l in range(depth):
        h = _ffn(h, ffn1_pre_g[l][None, :], ffn1_w_gate[l].astype(BF16), ffn1_w_up[l].astype(BF16),
                 ffn1_w_down[l].astype(BF16), ffn1_post_g[l][None, :])
        gates, act, dt = _inproj(h, mix_pre_g[l][None, :], _reorder_w_in(w_in[l]),
                                         conv_w[l].astype(F32), conv_b[l].astype(F32)[None, :],
                                         _pad_heads(dt_bias[l]), seq)
        h = _mixer(h, gates, act, dt, seq, _pad_heads(a_log[l]),
                   jnp.repeat(d_skip[l].astype(F32), SSM_HEAD_DIM)[None, :], ssm_norm_g[l][None, :],
                   w_ssm_proj[l].astype(BF16), w_attn_proj[l].astype(BF16), w_out[l].astype(BF16),
                   mix_post_g[l][None, :], attn_sinks[l].astype(F32) * LOG2E, table)
        h = _ffn(h, ffn2_pre_g[l][None, :], ffn2_w_gate[l].astype(BF16), ffn2_w_up[l].astype(BF16),
                 ffn2_w_down[l].astype(BF16), ffn2_post_g[l][None, :])
    return h.reshape(batch, seq, D_MODEL)
```

```python
import functools
import math

import numpy as np
import jax
import jax.numpy as jnp
from jax import lax
from jax.experimental import pallas as pl
from jax.experimental.pallas import tpu as pltpu

F32 = jnp.float32
BF16 = jnp.bfloat16

D_MODEL = 1024
D_FF = 2816
FFN_RESIDUAL_WEIGHT = 0.5
SSM_D_INNER = 2 * D_MODEL
SSM_HEAD_DIM = 64
SSM_HEADS = SSM_D_INNER // SSM_HEAD_DIM
SSM_GROUPS = 4
SSM_STATE = 128
SSM_CONV = 4
SSM_CHUNK = 128
SSM_BC_DIM = SSM_GROUPS * SSM_STATE
SSM_CONV_DIM = SSM_D_INNER + 2 * SSM_BC_DIM
SSM_GROUP_WIDTH = SSM_D_INNER // SSM_GROUPS
ATTN_Q_HEADS = 16
ATTN_KV_HEADS = 4
ATTN_HEAD_DIM = 64
ATTN_WINDOW = 128
ATTN_BLOCK = 128
ATTN_Q_DIM = ATTN_Q_HEADS * ATTN_HEAD_DIM
ATTN_KV_DIM = ATTN_KV_HEADS * ATTN_HEAD_DIM
REL_BUCKETS = 32
REL_MAX_DISTANCE = 128
RMS_EPS = 1e-6
LOG2E = math.log2(math.e)

LANES = 128
SUBLANES = 8
MXU_WIDTH = 256
DT_PAD = LANES
CONV_TAIL = SUBLANES
CONV_SLAB = MXU_WIDTH
CONV_BUFFERS = 4

SLAB_HEADS = MXU_WIDTH // SSM_HEAD_DIM
SSM_SLABS = SSM_D_INNER // MXU_WIDTH
SLABS_PER_GROUP = SSM_SLABS // SSM_GROUPS
ATTN_PAIRS = ATTN_Q_DIM // LANES
PAIRS_PER_KV = ATTN_PAIRS // ATTN_KV_HEADS
KV_DUP_DIM = ATTN_KV_HEADS * LANES

COL_GATES = 0
COL_Z = COL_GATES + 2 * D_MODEL
COL_XBC = COL_Z + SSM_D_INNER
COL_QKV = COL_XBC + SSM_CONV_DIM
QKV_DIM = ATTN_Q_DIM + 2 * KV_DUP_DIM
COL_DT = COL_QKV + QKV_DIM
IN_COLS_PADDED = COL_DT + DT_PAD
ACT_DIM = SSM_D_INNER + SSM_CONV_DIM + QKV_DIM

FFN_ROWS = 1024
FFN_SUB_ROWS = 512
INPROJ_ROWS = 512
MIX_ROWS = 512
PROJ_ROWS = 256
VMEM_LIMIT = 56 * 1024 * 1024
MIXER_VMEM_LIMIT = 60 * 1024 * 1024


def _rms(x, g):
    ms = jnp.mean(x * x, axis=-1, keepdims=True)
    return x * lax.rsqrt(ms + RMS_EPS) * g


def _sigmoid(x):
    return 0.5 * jnp.tanh(0.5 * x) + 0.5


def _silu(x):
    return x * _sigmoid(x)


def _softplus(x):
    return jnp.maximum(x, 0.0) + jnp.log1p(jnp.exp(-jnp.abs(x)))


def _dot(a, b):
    return jnp.dot(a, b, preferred_element_type=F32)


def _dot_nt(a, b):
    return lax.dot_general(a, b, (((1,), (1,)), ((), ())), preferred_element_type=F32)


def _split3(x):
    hi = x.astype(BF16)
    r1 = x - hi.astype(F32)
    mid = r1.astype(BF16)
    lo = (r1 - mid.astype(F32)).astype(BF16)
    return hi, mid, lo


def _split_act(act_ref):
    return (act_ref.at[:, 0:SSM_D_INNER], act_ref.at[:, SSM_D_INNER:SSM_D_INNER + SSM_CONV_DIM],
            act_ref.at[:, SSM_D_INNER + SSM_CONV_DIM:ACT_DIM])


def _resident(shape):
    return pl.BlockSpec(shape, lambda *_: (0,) * len(shape), pipeline_mode=pl.Buffered(1))


def _ffn_kernel(x_ref, pre_ref, wg_ref, wu_ref, wd_ref, post_ref, o_ref):
    for r0 in range(0, x_ref.shape[0], FFN_SUB_ROWS):
        rows = slice(r0, r0 + FFN_SUB_ROWS)
        x = x_ref[rows, :]
        u = _rms(x, pre_ref[...]).astype(BF16)
        g = _dot(u, wg_ref[...])
        up = _dot(u, wu_ref[...])
        hmid = (_silu(g) * up).astype(BF16)
        f = _dot(hmid, wd_ref[...])
        o_ref[rows, :] = x + FFN_RESIDUAL_WEIGHT * _rms(f, post_ref[...])


def _ffn(x2d, pre_g, wg, wu, wd, post_g):
    t = x2d.shape[0]
    row = pl.BlockSpec((FFN_ROWS, D_MODEL), lambda i: (i, 0))
    return pl.pallas_call(
        _ffn_kernel,
        grid=(t // FFN_ROWS,),
        in_specs=[row, _resident((1, D_MODEL)), _resident((D_MODEL, D_FF)), _resident((D_MODEL, D_FF)),
                  _resident((D_FF, D_MODEL)), _resident((1, D_MODEL))],
        out_specs=row,
        out_shape=jax.ShapeDtypeStruct((t, D_MODEL), F32),
        compiler_params=pltpu.CompilerParams(dimension_semantics=("arbitrary",),
                                             vmem_limit_bytes=VMEM_LIMIT),
        name="ffn",
    )(x2d, pre_g, wg, wu, wd, post_g)


def _inproj_kernel(seq, h_ref, g_ref, w_ref, convw_ref, convb_ref, dtb_ref,
                   gates_o, act_o, dt_o, xpad, tail):
    rows = h_ref.shape[0]
    i = pl.program_id(0)
    z_o, xbc_o, qkv_o = _split_act(act_o)

    @pl.when((i * rows) % seq == 0)
    def _new_sequence():
        tail[...] = jnp.zeros_like(tail)

    u = _rms(h_ref[...], g_ref[...]).astype(BF16)

    def conv_slab(n):
        cols = slice(n * CONV_SLAB, (n + 1) * CONV_SLAB)
        buf = xpad.at[n % CONV_BUFFERS]
        buf[0:CONV_TAIL, :] = tail[:, cols]
        buf[CONV_TAIL:CONV_TAIL + rows, :] = _dot(u, w_ref[:, COL_XBC + n * CONV_SLAB:COL_XBC + (n + 1) * CONV_SLAB])
        conv = convb_ref[:, cols]
        for k in range(SSM_CONV):
            start = CONV_TAIL - (SSM_CONV - 1) + k
            conv = conv + convw_ref[k:k + 1, cols] * buf[start:start + rows, :]
        xbc_o[:, cols] = _silu(conv).astype(BF16)
        tail[:, cols] = buf[rows:rows + CONV_TAIL, :]

    def piece(out_ref, col0, n, act):
        cols = slice(n * CONV_SLAB, (n + 1) * CONV_SLAB)
        out_ref[:, cols] = act(_dot(u, w_ref[:, col0 + n * CONV_SLAB:col0 + (n + 1) * CONV_SLAB])).astype(BF16)

    def dt_piece():
        dt_o[...] = _softplus(_dot(u, w_ref[:, COL_DT:IN_COLS_PADDED]) + dtb_ref[...])

    others = []
    for n in range(QKV_DIM // CONV_SLAB):
        others.append(functools.partial(piece, gates_o, COL_GATES, n, _sigmoid))
        others.append(functools.partial(piece, qkv_o, COL_QKV, n, lambda v: v))
        others.append(functools.partial(piece, z_o, COL_Z, n, _silu))
    others.append(dt_piece)
    n_conv = SSM_CONV_DIM // CONV_SLAB
    for n in range(n_conv):
        conv_slab(n)
        for k in range(n * len(others) // n_conv, (n + 1) * len(others) // n_conv):
            others[k]()


def _inproj(h2d, g, w, conv_w, conv_b, dt_bias, seq):
    t = h2d.shape[0]

    def row(width):
        return pl.BlockSpec((INPROJ_ROWS, width), lambda i: (i, 0))

    widths = (2 * D_MODEL, ACT_DIM, DT_PAD)
    dtypes = (BF16, BF16, F32)
    return pl.pallas_call(
        functools.partial(_inproj_kernel, seq),
        grid=(t // INPROJ_ROWS,),
        in_specs=[row(D_MODEL), _resident((1, D_MODEL)), _resident((D_MODEL, IN_COLS_PADDED)),
                  _resident((SSM_CONV, SSM_CONV_DIM)), _resident((1, SSM_CONV_DIM)), _resident((1, DT_PAD))],
        out_specs=[row(w_) for w_ in widths],
        out_shape=[jax.ShapeDtypeStruct((t, w_), d_) for w_, d_ in zip(widths, dtypes)],
        scratch_shapes=[pltpu.VMEM((CONV_BUFFERS, CONV_TAIL + INPROJ_ROWS, CONV_SLAB), F32),
                        pltpu.VMEM((CONV_TAIL, SSM_CONV_DIM), F32)],
        compiler_params=pltpu.CompilerParams(dimension_semantics=("arbitrary",),
                                             vmem_limit_bytes=VMEM_LIMIT),
        name="inproj",
    )(h2d, g, w, conv_w, conv_b, dt_bias)


def _mixer_kernel(tiles_per_seq, sinks_ref, table_ref,
                  h_ref, gates_ref, act_ref, dt_ref,
                  alog_ref, dexp_ref, ng_ref, bucket_ref,
                  wssm_ref, wattn_ref, wout_ref, postg_ref,
                  o_ref,
                  kvpad, yscr, ynorm, attn, pssm, pmerged, pmix, state, bias):
    z_ref, xbc_ref, qkv_ref = _split_act(act_ref)
    rows_per_step = act_ref.shape[0]
    n_chunks = rows_per_step // SSM_CHUNK
    s = pl.program_id(0)
    q = SSM_CHUNK
    cur_base = (s % 2) * rows_per_step
    prev_base = rows_per_step - cur_base

    def cur_rows(c):
        return pl.ds(pl.multiple_of(cur_base + c * q, q), q)


    ii = lax.broadcasted_iota(jnp.int32, (q, q), 0)
    jj = lax.broadcasted_iota(jnp.int32, (q, q), 1)
    causal = ii >= jj
    tril3 = jnp.concatenate([jnp.where(causal, 1.0, 0.0).astype(BF16)] * 3, axis=1)
    low_half = jj < SSM_HEAD_DIM
    low_half_row = low_half[0:1, :]
    slab_lane_head = lax.broadcasted_iota(jnp.int32, (q, MXU_WIDTH), 1) // SSM_HEAD_DIM
    band_low_half = lax.broadcasted_iota(jnp.int32, (2 * ATTN_BLOCK, LANES), 1) < ATTN_HEAD_DIM

    @pl.when(s == 0)
    def _init():
        bi = lax.broadcasted_iota(jnp.int32, (ATTN_BLOCK, 2 * ATTN_BLOCK), 0)
        bj = lax.broadcasted_iota(jnp.int32, (ATTN_BLOCK, 2 * ATTN_BLOCK), 1)
        dist = bi + ATTN_BLOCK - bj
        in_window = (dist >= 0) & (dist < ATTN_WINDOW)
        bucket = bucket_ref[...]
        for hq in range(ATTN_Q_HEADS):
            acc = jnp.zeros((ATTN_BLOCK, 2 * ATTN_BLOCK), F32)
            for bk in range(REL_BUCKETS):
                acc = jnp.where(bucket == bk, table_ref[bk, hq], acc)
            bias[hq] = jnp.where(in_window, acc, -jnp.inf)
        ynorm[rows_per_step:2 * rows_per_step, :] = jnp.zeros((rows_per_step, SSM_D_INNER), BF16)
        attn[rows_per_step:2 * rows_per_step, :] = jnp.zeros((rows_per_step, ATTN_Q_DIM), BF16)

    new_sequence = s % tiles_per_seq == 0

    @pl.when(new_sequence)
    def _reset_sequence_state():
        state[...] = jnp.zeros_like(state)
        kvpad[0:ATTN_BLOCK, :] = jnp.zeros((ATTN_BLOCK, 2 * KV_DUP_DIM), BF16)

    kvpad[ATTN_BLOCK:ATTN_BLOCK + rows_per_step, :] = qkv_ref[:, ATTN_Q_DIM:QKV_DIM]

    a_neg = -jnp.exp(alog_ref[...])

    pending = []
    for blk in range(rows_per_step // PROJ_ROWS):
        brows = slice(blk * PROJ_ROWS, (blk + 1) * PROJ_ROWS)
        prows = pl.ds(pl.multiple_of(prev_base + blk * PROJ_ROWS, PROJ_ROWS), PROJ_ROWS)
        for n in range(D_MODEL // MXU_WIDTH):
            ns = slice(n * MXU_WIDTH, (n + 1) * MXU_WIDTH)
            ns_attn = slice(D_MODEL + n * MXU_WIDTH, D_MODEL + (n + 1) * MXU_WIDTH)

            def ssm_item(ns=ns, prows=prows):
                pssm[:, ns] = _dot(ynorm[prows, :], wssm_ref[:, ns])

            def attn_merge_item(ns=ns, ns_attn=ns_attn, prows=prows, brows=brows):
                y_attn = _dot(attn[prows, :], wattn_ref[:, ns])
                merged = (gates_ref[brows, ns].astype(F32) * pssm[:, ns]
                          + gates_ref[brows, ns_attn].astype(F32) * y_attn)
                pmerged[:, ns] = merged.astype(BF16)

            pending += [ssm_item, attn_merge_item]
        for n in range(D_MODEL // MXU_WIDTH):
            ns = slice(n * MXU_WIDTH, (n + 1) * MXU_WIDTH)

            def out_item(ns=ns):
                pmix[:, ns] = _dot(pmerged[...], wout_ref[:, ns])

            pending.append(out_item)

        def residual_item(brows=brows):
            o_ref[brows, :] = h_ref[brows, :] + _rms(pmix[...], postg_ref[...])

        pending.append(residual_item)
    n_slots = n_chunks * 2 * SSM_SLABS
    assert len(pending) <= n_slots
    issue_at = {(k * n_slots) // len(pending) for k in range(len(pending))}
    slot = [0]

    def issue_pending():
        if slot[0] in issue_at:
            pending.pop(0)()
        slot[0] += 1

    for c in range(n_chunks):
        rows = slice(c * q, (c + 1) * q)

        dt = dt_ref[rows, :]
        hi, mid, lo = _split3(dt * a_neg)
        acs2 = _dot(tril3, jnp.concatenate([hi, mid, lo], axis=0)) * LOG2E
        row2_t = (acs2 - jnp.log2(dt)).T

        band = slice(c * q, c * q + 2 * ATTN_BLOCK)
        if c == 0:
            key_idx = lax.broadcasted_iota(jnp.int32, (ATTN_BLOCK, 2 * ATTN_BLOCK), 1)
            no_prev_block = jnp.where(new_sequence & (key_idx < ATTN_BLOCK), -jnp.inf, 0.0)

        def attention_logits(pair):
            kv = pair // PAIRS_PER_KV
            kd = kvpad[band, kv * LANES:(kv + 1) * LANES]
            zero = jnp.zeros_like(kd)
            k_cat = jnp.concatenate([jnp.where(band_low_half, kd, zero), jnp.where(band_low_half, zero, kd)], axis=0)
            q_pair = qkv_ref[rows, pair * LANES:(pair + 1) * LANES]
            return _dot_nt(q_pair, k_cat)

        def attention_output(pair, logits):
            kv = pair // PAIRS_PER_KV
            vd = kvpad[band, KV_DUP_DIM + kv * LANES:KV_DUP_DIM + (kv + 1) * LANES]
            zero = jnp.zeros_like(vd)
            v_cat = jnp.concatenate([jnp.where(band_low_half, vd, zero), jnp.where(band_low_half, zero, vd)], axis=0)
            ps, invs = [], []
            for half in range(2):
                hq = 2 * pair + half
                l2 = logits[:, half * 2 * ATTN_BLOCK:(half + 1) * 2 * ATTN_BLOCK] + bias[hq]
                if c == 0:
                    l2 = l2 + no_prev_block
                sink = sinks_ref[hq]
                mx = jnp.maximum(jnp.max(l2, axis=-1, keepdims=True), sink)
                p = jnp.exp2(l2 - mx)
                den = jnp.sum(p, axis=-1, keepdims=True) + jnp.exp2(sink - mx)
                ps.append(p.astype(BF16))
                invs.append(1.0 / den)
            o = _dot(jnp.concatenate(ps, axis=1), v_cat)
            o = o * jnp.where(low_half, invs[0], invs[1])
            attn[cur_rows(c), pair * LANES:(pair + 1) * LANES] = o.astype(BF16)

        logits_next = attention_logits(0)
        for m in range(SSM_SLABS):
            logits_m = logits_next
            issue_pending()
            g = m // SLABS_PER_GROUP
            if m % SLABS_PER_GROUP == 0:
                b_g = xbc_ref[rows, SSM_D_INNER + g * SSM_STATE:SSM_D_INNER + (g + 1) * SSM_STATE]
                c_g = xbc_ref[rows, SSM_D_INNER + SSM_BC_DIM + g * SSM_STATE:
                              SSM_D_INNER + SSM_BC_DIM + (g + 1) * SSM_STATE]
                scores = _dot_nt(c_g, b_g)
                b_gt = b_g.astype(F32).T
            y_off = _dot(c_g, state[m].astype(BF16))
            if m + 1 < ATTN_PAIRS:
                logits_next = attention_logits(m + 1)
            tops, bots, y_off_scaled, chunk_decay = [], [], [], []
            for r0 in range(0, SLAB_HEADS, 2):
                ebs, cds = [], []
                for r in (r0, r0 + 1):
                    h = m * SLAB_HEADS + r
                    colb = jnp.broadcast_to(acs2[:, h:h + 1], (q, q))
                    row2 = row2_t[h:h + 1, :]
                    lastb = colb[q - 1:q, :]
                    l_mat = jnp.exp2(jnp.where(causal, colb - row2, -jnp.inf)) * scores
                    tops.append(l_mat.astype(BF16))
                    bots.append((b_gt * jnp.exp2(lastb - row2)).astype(BF16))
                    ebs.append(jnp.exp2(colb))
                    cds.append(jnp.exp2(lastb))
                lanes = slice(r0 * SSM_HEAD_DIM, (r0 + 2) * SSM_HEAD_DIM)
                y_off_scaled.append(y_off[:, lanes] * jnp.where(low_half, ebs[0], ebs[1]))
                chunk_decay.append(jnp.where(low_half_row, cds[0], cds[1]))
            lhs = jnp.concatenate([jnp.concatenate(tops, axis=1), jnp.concatenate(bots, axis=1)], axis=0)
            xs_slab = xbc_ref[rows, m * MXU_WIDTH:(m + 1) * MXU_WIDTH]
            rhs = jnp.concatenate(
                [jnp.where(slab_lane_head == r, xs_slab, jnp.zeros_like(xs_slab)) for r in range(SLAB_HEADS)],
                axis=0)
            yy = _dot(lhs, rhs)
            yscr[:, m * MXU_WIDTH:(m + 1) * MXU_WIDTH] = yy[0:q] + jnp.concatenate(y_off_scaled, axis=1)
            state[m] = state[m] * jnp.concatenate(chunk_decay, axis=1) + yy[q:2 * q]
            issue_pending()
            attention_output(m, logits_m)

        xs = xbc_ref[rows, 0:SSM_D_INNER].astype(F32)
        y = yscr[...] + dexp_ref[...] * xs
        yg = y * z_ref[rows, :].astype(F32)
        for g in range(SSM_GROUPS):
            gl = slice(g * SSM_GROUP_WIDTH, (g + 1) * SSM_GROUP_WIDTH)
            v = yg[:, gl]
            ms = jnp.mean(v * v, axis=-1, keepdims=True)
            ynorm[cur_rows(c), gl] = (v * lax.rsqrt(ms + RMS_EPS) * ng_ref[:, gl]).astype(BF16)

    assert not pending

    kvpad[0:ATTN_BLOCK, :] = kvpad[rows_per_step:rows_per_step + ATTN_BLOCK, :]


def _t5_bucket_matrix():
    qi = np.arange(ATTN_BLOCK)[:, None]
    kj = np.arange(2 * ATTN_BLOCK)[None, :]
    dist = np.maximum(qi + ATTN_BLOCK - kj, 0).astype(np.int32)
    max_exact = REL_BUCKETS // 2
    d = np.maximum(dist, 1).astype(np.float32)
    large = max_exact + (np.log(d / np.float32(max_exact)) / np.float32(math.log(REL_MAX_DISTANCE / max_exact))
                         * np.float32(REL_BUCKETS - max_exact)).astype(np.int32)
    large = np.minimum(large, REL_BUCKETS - 1)
    return np.where(dist < max_exact, dist, large).astype(np.int32)


def _mixer(h2d, gates, act, dt, seq, a_log, d_exp, norm_g, w_ssm, w_attn, w_out, post_g, sinks, table):
    n_tiles = h2d.shape[0] // MIX_ROWS

    def scanned(width):
        return pl.BlockSpec((MIX_ROWS, width), lambda s: (jnp.minimum(s, n_tiles - 1), 0))

    def finished(width):
        return pl.BlockSpec((MIX_ROWS, width), lambda s: (jnp.maximum(s - 1, 0), 0))

    smem = pl.BlockSpec(memory_space=pltpu.SMEM)
    bucket = jnp.asarray(_t5_bucket_matrix())
    in_specs = [
        smem, smem,
        finished(D_MODEL), finished(2 * D_MODEL),
        scanned(ACT_DIM), scanned(DT_PAD),
        _resident((1, DT_PAD)), _resident((1, SSM_D_INNER)), _resident((1, SSM_D_INNER)),
        _resident((ATTN_BLOCK, 2 * ATTN_BLOCK)),
        _resident((SSM_D_INNER, D_MODEL)), _resident((ATTN_Q_DIM, D_MODEL)), _resident((D_MODEL, D_MODEL)),
        _resident((1, D_MODEL)),
    ]
    scratch = [
        pltpu.VMEM((ATTN_BLOCK + MIX_ROWS, 2 * KV_DUP_DIM), BF16),
        pltpu.VMEM((SSM_CHUNK, SSM_D_INNER), F32),
        pltpu.VMEM((2 * MIX_ROWS, SSM_D_INNER), BF16),
        pltpu.VMEM((2 * MIX_ROWS, ATTN_Q_DIM), BF16),
        pltpu.VMEM((PROJ_ROWS, D_MODEL), F32),
        pltpu.VMEM((PROJ_ROWS, D_MODEL), BF16),
        pltpu.VMEM((PROJ_ROWS, D_MODEL), F32),
        pltpu.VMEM((SSM_SLABS, SSM_STATE, MXU_WIDTH), F32),
        pltpu.VMEM((ATTN_Q_HEADS, ATTN_BLOCK, 2 * ATTN_BLOCK), F32),
    ]
    return pl.pallas_call(
        functools.partial(_mixer_kernel, seq // MIX_ROWS),
        grid=(n_tiles + 1,),
        in_specs=in_specs,
        out_specs=finished(D_MODEL),
        out_shape=jax.ShapeDtypeStruct(h2d.shape, F32),
        scratch_shapes=scratch,
        compiler_params=pltpu.CompilerParams(dimension_semantics=("arbitrary",),
                                             vmem_limit_bytes=MIXER_VMEM_LIMIT),
        name="mixer",
    )(sinks, table, h2d, gates, act, dt, a_log, d_exp, norm_g, bucket,
      w_ssm, w_attn, w_out, post_g)


def _reorder_w_in(w_in):
    o_z = 2 * D_MODEL
    o_dt = o_z + SSM_D_INNER + SSM_CONV_DIM
    o_q = o_dt + SSM_HEADS
    o_k = o_q + ATTN_Q_DIM
    o_v = o_k + ATTN_KV_DIM

    def dup_heads(w):
        w = w.reshape(D_MODEL, ATTN_KV_HEADS, 1, ATTN_HEAD_DIM)
        return jnp.broadcast_to(w, (D_MODEL, ATTN_KV_HEADS, 2, ATTN_HEAD_DIM)).reshape(D_MODEL, KV_DUP_DIM)

    w_q = w_in[:, o_q:o_k] * (ATTN_HEAD_DIM ** -0.5 * LOG2E)
    dt_cols = jnp.pad(w_in[:, o_dt:o_q], ((0, 0), (0, DT_PAD - SSM_HEADS)))
    parts = [w_in[:, :o_dt], w_q, dup_heads(w_in[:, o_k:o_v]), dup_heads(w_in[:, o_v:]), dt_cols]
    return jnp.concatenate([p.astype(BF16) for p in parts], axis=1)


def _pad_heads(v):
    return jnp.pad(v.astype(F32), (0, DT_PAD - SSM_HEADS))[None, :]


def kernel(x, ffn1_pre_g, ffn1_w_gate, ffn1_w_up, ffn1_w_down, ffn1_post_g, mix_pre_g, w_in, conv_w, conv_b, dt_bias, a_log, d_skip, ssm_norm_g, w_ssm_proj, attn_sinks, rel_bias_table, w_attn_proj, w_out, mix_post_g, ffn2_pre_g, ffn2_w_gate, ffn2_w_up, ffn2_w_down, ffn2_post_g):
    batch, seq, _ = x.shape
    depth = w_in.shape[0]
    h = x.reshape(batch * seq, D_MODEL)
    table = rel_bias_table.astype(F32) * LOG2E
    for l in range(depth):
        h = _ffn(h, ffn1_pre_g[l][None, :], ffn1_w_gate[l].astype(BF16), ffn1_w_up[l].astype(BF16),
                 ffn1_w_down[l].astype(BF16), ffn1_post_g[l][None, :])
        gates, act, dt = _inproj(h, mix_pre_g[l][None, :], _reorder_w_in(w_in[l]),
                                         conv_w[l].astype(F32), conv_b[l].astype(F32)[None, :],
                                         _pad_heads(dt_bias[l]), seq)
        h = _mixer(h, gates, act, dt, seq, _pad_heads(a_log[l]),
                   jnp.repeat(d_skip[l].astype(F32), SSM_HEAD_DIM)[None, :], ssm_norm_g[l][None, :],
                   w_ssm_proj[l].astype(BF16), w_attn_proj[l].astype(BF16), w_out[l].astype(BF16),
                   mix_post_g[l][None, :], attn_sinks[l].astype(F32) * LOG2E, table)
        h = _ffn(h, ffn2_pre_g[l][None, :], ffn2_w_gate[l].astype(BF16), ffn2_w_up[l].astype(BF16),
                 ffn2_w_down[l].astype(BF16), ffn2_post_g[l][None, :])
    return h.reshape(batch, seq, D_MODEL)
```

```python
import functools
import math

import numpy as np
import jax
import jax.numpy as jnp
from jax import lax
from jax.experimental import pallas as pl
from jax.experimental.pallas import tpu as pltpu

F32 = jnp.float32
BF16 = jnp.bfloat16

D_MODEL = 1024
D_FF = 2816
FFN_RESIDUAL_WEIGHT = 0.5
SSM_D_INNER = 2 * D_MODEL
SSM_HEAD_DIM = 64
SSM_HEADS = SSM_D_INNER // SSM_HEAD_DIM
SSM_GROUPS = 4
SSM_STATE = 128
SSM_CONV = 4
SSM_CHUNK = 128
SSM_BC_DIM = SSM_GROUPS * SSM_STATE
SSM_CONV_DIM = SSM_D_INNER + 2 * SSM_BC_DIM
SSM_GROUP_WIDTH = SSM_D_INNER // SSM_GROUPS
ATTN_Q_HEADS = 16
ATTN_KV_HEADS = 4
ATTN_HEAD_DIM = 64
ATTN_WINDOW = 128
ATTN_BLOCK = 128
ATTN_Q_DIM = ATTN_Q_HEADS * ATTN_HEAD_DIM
ATTN_KV_DIM = ATTN_KV_HEADS * ATTN_HEAD_DIM
REL_BUCKETS = 32
REL_MAX_DISTANCE = 128
RMS_EPS = 1e-6
LOG2E = math.log2(math.e)

LANES = 128
SUBLANES = 8
MXU_WIDTH = 256
DT_PAD = LANES
CONV_TAIL = SUBLANES
CONV_SLAB = MXU_WIDTH
CONV_BUFFERS = 4

SLAB_HEADS = MXU_WIDTH // SSM_HEAD_DIM
SSM_SLABS = SSM_D_INNER // MXU_WIDTH
SLABS_PER_GROUP = SSM_SLABS // SSM_GROUPS
ATTN_PAIRS = ATTN_Q_DIM // LANES
PAIRS_PER_KV = ATTN_PAIRS // ATTN_KV_HEADS
KV_DUP_DIM = ATTN_KV_HEADS * LANES

COL_GATES = 0
COL_Z = COL_GATES + 2 * D_MODEL
COL_XBC = COL_Z + SSM_D_INNER
COL_QKV = COL_XBC + SSM_CONV_DIM
QKV_DIM = ATTN_Q_DIM + 2 * KV_DUP_DIM
COL_DT = COL_QKV + QKV_DIM
IN_COLS_PADDED = COL_DT + DT_PAD
ACT_DIM = SSM_D_INNER + SSM_CONV_DIM + QKV_DIM

FFN_ROWS = 1024
FFN_SUB_ROWS = 512
INPROJ_ROWS = 512
MIX_ROWS = 512
PROJ_ROWS = 256
VMEM_LIMIT = 56 * 1024 * 1024
MIXER_VMEM_LIMIT = 60 * 1024 * 1024


def _rms(x, g):
    ms = jnp.mean(x * x, axis=-1, keepdims=True)
    return x * lax.rsqrt(ms + RMS_EPS) * g


def _sigmoid_of_half(xh):
    return 0.5 * (jnp.tanh(xh) + 1.0)


def _silu_of_half(xh):
    return xh * (jnp.tanh(xh) + 1.0)


def _softplus(x):
    return jnp.maximum(x, 0.0) + jnp.log1p(jnp.exp(-jnp.abs(x)))


def _dot(a, b):
    return jnp.dot(a, b, preferred_element_type=F32)


def _dot_nt(a, b):
    return lax.dot_general(a, b, (((1,), (1,)), ((), ())), preferred_element_type=F32)


def _split3(x):
    hi = x.astype(BF16)
    r1 = x - hi.astype(F32)
    mid = r1.astype(BF16)
    lo = (r1 - mid.astype(F32)).astype(BF16)
    return hi, mid, lo


def _split_act(act_ref):
    return (act_ref.at[:, 0:SSM_D_INNER], act_ref.at[:, SSM_D_INNER:SSM_D_INNER + SSM_CONV_DIM],
            act_ref.at[:, SSM_D_INNER + SSM_CONV_DIM:ACT_DIM])


def _resident(shape):
    return pl.BlockSpec(shape, lambda *_: (0,) * len(shape), pipeline_mode=pl.Buffered(1))


def _ffn_kernel(x_ref, pre_ref, wg_ref, wu_ref, wd_ref, post_ref, o_ref):
    for r0 in range(0, x_ref.shape[0], FFN_SUB_ROWS):
        rows = slice(r0, r0 + FFN_SUB_ROWS)
        x = x_ref[rows, :]
        u = _rms(x, pre_ref[...]).astype(BF16)
        g = _dot(u, wg_ref[...])
        up = _dot(u, wu_ref[...])
        hmid = (_silu_of_half(g) * up).astype(BF16)
        f = _dot(hmid, wd_ref[...])
        o_ref[rows, :] = x + FFN_RESIDUAL_WEIGHT * _rms(f, post_ref[...])


def _ffn(x2d, pre_g, wg, wu, wd, post_g):
    t = x2d.shape[0]
    row = pl.BlockSpec((FFN_ROWS, D_MODEL), lambda i: (i, 0))
    return pl.pallas_call(
        _ffn_kernel,
        grid=(t // FFN_ROWS,),
        in_specs=[row, _resident((1, D_MODEL)), _resident((D_MODEL, D_FF)), _resident((D_MODEL, D_FF)),
                  _resident((D_FF, D_MODEL)), _resident((1, D_MODEL))],
        out_specs=row,
        out_shape=jax.ShapeDtypeStruct((t, D_MODEL), F32),
        compiler_params=pltpu.CompilerParams(dimension_semantics=("arbitrary",),
                                             vmem_limit_bytes=VMEM_LIMIT),
        name="ffn",
    )(x2d, pre_g, wg, wu, wd, post_g)


def _inproj_kernel(seq, h_ref, g_ref, w_ref, convw_ref, convb_ref, dtb_ref,
                   gates_o, act_o, dt_o, xpad, tail):
    rows = h_ref.shape[0]
    i = pl.program_id(0)
    z_o, xbc_o, qkv_o = _split_act(act_o)

    @pl.when((i * rows) % seq == 0)
    def _new_sequence():
        tail[...] = jnp.zeros_like(tail)

    u = _rms(h_ref[...], g_ref[...]).astype(BF16)

    def conv_slab(n):
        cols = slice(n * CONV_SLAB, (n + 1) * CONV_SLAB)
        buf = xpad.at[n % CONV_BUFFERS]
        buf[0:CONV_TAIL, :] = tail[:, cols]
        buf[CONV_TAIL:CONV_TAIL + rows, :] = _dot(u, w_ref[:, COL_XBC + n * CONV_SLAB:COL_XBC + (n + 1) * CONV_SLAB])
        conv = convb_ref[:, cols]
        for k in range(SSM_CONV):
            start = CONV_TAIL - (SSM_CONV - 1) + k
            conv = conv + convw_ref[k:k + 1, cols] * buf[start:start + rows, :]
        xbc_o[:, cols] = _silu_of_half(conv).astype(BF16)
        tail[:, cols] = buf[rows:rows + CONV_TAIL, :]

    def piece(out_ref, col0, n, act):
        cols = slice(n * CONV_SLAB, (n + 1) * CONV_SLAB)
        out_ref[:, cols] = act(_dot(u, w_ref[:, col0 + n * CONV_SLAB:col0 + (n + 1) * CONV_SLAB])).astype(BF16)

    def dt_piece():
        dt_o[...] = _softplus(_dot(u, w_ref[:, COL_DT:IN_COLS_PADDED]) + dtb_ref[...])

    others = []
    for n in range(QKV_DIM // CONV_SLAB):
        others.append(functools.partial(piece, gates_o, COL_GATES, n, _sigmoid_of_half))
        others.append(functools.partial(piece, qkv_o, COL_QKV, n, lambda v: v))
        others.append(functools.partial(piece, z_o, COL_Z, n, _silu_of_half))
    others.append(dt_piece)
    n_conv = SSM_CONV_DIM // CONV_SLAB
    for n in range(n_conv):
        conv_slab(n)
        for k in range(n * len(others) // n_conv, (n + 1) * len(others) // n_conv):
            others[k]()


def _inproj(h2d, g, w, conv_w, conv_b, dt_bias, seq):
    t = h2d.shape[0]

    def row(width):
        return pl.BlockSpec((INPROJ_ROWS, width), lambda i: (i, 0))

    widths = (2 * D_MODEL, ACT_DIM, DT_PAD)
    dtypes = (BF16, BF16, F32)
    return pl.pallas_call(
        functools.partial(_inproj_kernel, seq),
        grid=(t // INPROJ_ROWS,),
        in_specs=[row(D_MODEL), _resident((1, D_MODEL)), _resident((D_MODEL, IN_COLS_PADDED)),
                  _resident((SSM_CONV, SSM_CONV_DIM)), _resident((1, SSM_CONV_DIM)), _resident((1, DT_PAD))],
        out_specs=[row(w_) for w_ in widths],
        out_shape=[jax.ShapeDtypeStruct((t, w_), d_) for w_, d_ in zip(widths, dtypes)],
        scratch_shapes=[pltpu.VMEM((CONV_BUFFERS, CONV_TAIL + INPROJ_ROWS, CONV_SLAB), F32),
                        pltpu.VMEM((CONV_TAIL, SSM_CONV_DIM), F32)],
        compiler_params=pltpu.CompilerParams(dimension_semantics=("arbitrary",),
                                             vmem_limit_bytes=VMEM_LIMIT),
        name="inproj",
    )(h2d, g, w, conv_w, conv_b, dt_bias)


def _mixer_kernel(tiles_per_seq, sinks_ref, table_ref,
                  h_ref, gates_ref, act_ref, dt_ref,
                  alog_ref, dexp_ref, ng_ref, bucket_ref,
                  wssm_ref, wattn_ref, wout_ref, postg_ref,
                  o_ref,
                  kvpad, yscr, ynorm, attn, pssm, pmerged, pmix, state, bias):
    z_ref, xbc_ref, qkv_ref = _split_act(act_ref)
    rows_per_step = act_ref.shape[0]
    n_chunks = rows_per_step // SSM_CHUNK
    s = pl.program_id(0)
    q = SSM_CHUNK
    cur_base = (s % 2) * rows_per_step
    prev_base = rows_per_step - cur_base

    def cur_rows(c):
        return pl.ds(pl.multiple_of(cur_base + c * q, q), q)


    ii = lax.broadcasted_iota(jnp.int32, (q, q), 0)
    jj = lax.broadcasted_iota(jnp.int32, (q, q), 1)
    causal = ii >= jj
    tril3 = jnp.concatenate([jnp.where(causal, 1.0, 0.0).astype(BF16)] * 3, axis=1)
    low_half = jj < SSM_HEAD_DIM
    low_half_row = low_half[0:1, :]
    slab_lane_head = lax.broadcasted_iota(jnp.int32, (q, MXU_WIDTH), 1) // SSM_HEAD_DIM
    band_low_half = lax.broadcasted_iota(jnp.int32, (2 * ATTN_BLOCK, LANES), 1) < ATTN_HEAD_DIM

    @pl.when(s == 0)
    def _init():
        bi = lax.broadcasted_iota(jnp.int32, (ATTN_BLOCK, 2 * ATTN_BLOCK), 0)
        bj = lax.broadcasted_iota(jnp.int32, (ATTN_BLOCK, 2 * ATTN_BLOCK), 1)
        dist = bi + ATTN_BLOCK - bj
        in_window = (dist >= 0) & (dist < ATTN_WINDOW)
        bucket = bucket_ref[...]
        for hq in range(ATTN_Q_HEADS):
            acc = jnp.zeros((ATTN_BLOCK, 2 * ATTN_BLOCK), F32)
            for bk in range(REL_BUCKETS):
                acc = jnp.where(bucket == bk, table_ref[bk, hq], acc)
            bias[hq] = jnp.where(in_window, acc, -jnp.inf)
        ynorm[rows_per_step:2 * rows_per_step, :] = jnp.zeros((rows_per_step, SSM_D_INNER), BF16)
        attn[rows_per_step:2 * rows_per_step, :] = jnp.zeros((rows_per_step, ATTN_Q_DIM), BF16)

    new_sequence = s % tiles_per_seq == 0

    @pl.when(new_sequence)
    def _reset_sequence_state():
        state[...] = jnp.zeros_like(state)
        kvpad[0:ATTN_BLOCK, :] = jnp.zeros((ATTN_BLOCK, 2 * KV_DUP_DIM), BF16)

    kvpad[ATTN_BLOCK:ATTN_BLOCK + rows_per_step, :] = qkv_ref[:, ATTN_Q_DIM:QKV_DIM]

    a_neg = -jnp.exp(alog_ref[...])

    pending = []
    for blk in range(rows_per_step // PROJ_ROWS):
        brows = slice(blk * PROJ_ROWS, (blk + 1) * PROJ_ROWS)
        prows = pl.ds(pl.multiple_of(prev_base + blk * PROJ_ROWS, PROJ_ROWS), PROJ_ROWS)
        for n in range(D_MODEL // MXU_WIDTH):
            ns = slice(n * MXU_WIDTH, (n + 1) * MXU_WIDTH)
            ns_attn = slice(D_MODEL + n * MXU_WIDTH, D_MODEL + (n + 1) * MXU_WIDTH)

            def ssm_item(ns=ns, prows=prows):
                pssm[:, ns] = _dot(ynorm[prows, :], wssm_ref[:, ns])

            def attn_merge_item(ns=ns, ns_attn=ns_attn, prows=prows, brows=brows):
                y_attn = _dot(attn[prows, :], wattn_ref[:, ns])
                merged = (gates_ref[brows, ns].astype(F32) * pssm[:, ns]
                          + gates_ref[brows, ns_attn].astype(F32) * y_attn)
                pmerged[:, ns] = merged.astype(BF16)

            pending += [ssm_item, attn_merge_item]
        for n in range(D_MODEL // MXU_WIDTH):
            ns = slice(n * MXU_WIDTH, (n + 1) * MXU_WIDTH)

            def out_item(ns=ns):
                pmix[:, ns] = _dot(pmerged[...], wout_ref[:, ns])

            pending.append(out_item)

        def residual_item(brows=brows):
            o_ref[brows, :] = h_ref[brows, :] + _rms(pmix[...], postg_ref[...])

        pending.append(residual_item)
    n_slots = n_chunks * 2 * SSM_SLABS
    assert len(pending) <= n_slots
    issue_at = {(k * n_slots) // len(pending) for k in range(len(pending))}
    slot = [0]

    def issue_pending():
        if slot[0] in issue_at:
            pending.pop(0)()
        slot[0] += 1

    for c in range(n_chunks):
        rows = slice(c * q, (c + 1) * q)

        dt = dt_ref[rows, :]
        hi, mid, lo = _split3(dt * a_neg)
        acs2 = _dot(tril3, jnp.concatenate([hi, mid, lo], axis=0)) * LOG2E
        row2_t = (acs2 - jnp.log2(dt)).T

        band = slice(c * q, c * q + 2 * ATTN_BLOCK)
        if c == 0:
            key_idx = lax.broadcasted_iota(jnp.int32, (ATTN_BLOCK, 2 * ATTN_BLOCK), 1)
            no_prev_block = jnp.where(new_sequence & (key_idx < ATTN_BLOCK), -jnp.inf, 0.0)

        def attention_logits(pair):
            kv = pair // PAIRS_PER_KV
            kd = kvpad[band, kv * LANES:(kv + 1) * LANES]
            zero = jnp.zeros_like(kd)
            k_cat = jnp.concatenate([jnp.where(band_low_half, kd, zero), jnp.where(band_low_half, zero, kd)], axis=0)
            q_pair = qkv_ref[rows, pair * LANES:(pair + 1) * LANES]
            return _dot_nt(q_pair, k_cat)

        def attention_output(pair, logits):
            kv = pair // PAIRS_PER_KV
            vd = kvpad[band, KV_DUP_DIM + kv * LANES:KV_DUP_DIM + (kv + 1) * LANES]
            zero = jnp.zeros_like(vd)
            v_cat = jnp.concatenate([jnp.where(band_low_half, vd, zero), jnp.where(band_low_half, zero, vd)], axis=0)
            ps, invs = [], []
            for half in range(2):
                hq = 2 * pair + half
                l2 = logits[:, half * 2 * ATTN_BLOCK:(half + 1) * 2 * ATTN_BLOCK] + bias[hq]
                if c == 0:
                    l2 = l2 + no_prev_block
                sink = sinks_ref[hq]
                mx = jnp.maximum(jnp.max(l2, axis=-1, keepdims=True), sink)
                p = jnp.exp2(l2 - mx)
                den = jnp.sum(p, axis=-1, keepdims=True) + jnp.exp2(sink - mx)
                ps.append(p.astype(BF16))
                invs.append(1.0 / den)
            o = _dot(jnp.concatenate(ps, axis=1), v_cat)
            o = o * jnp.where(low_half, invs[0], invs[1])
            attn[cur_rows(c), pair * LANES:(pair + 1) * LANES] = o.astype(BF16)

        logits_next = attention_logits(0)
        for m in range(SSM_SLABS):
            logits_m = logits_next
            issue_pending()
            g = m // SLABS_PER_GROUP
            if m % SLABS_PER_GROUP == 0:
                b_g = xbc_ref[rows, SSM_D_INNER + g * SSM_STATE:SSM_D_INNER + (g + 1) * SSM_STATE]
                c_g = xbc_ref[rows, SSM_D_INNER + SSM_BC_DIM + g * SSM_STATE:
                              SSM_D_INNER + SSM_BC_DIM + (g + 1) * SSM_STATE]
                scores = _dot_nt(c_g, b_g)
                b_gt = b_g.astype(F32).T
            y_off = _dot(c_g, state[m].astype(BF16))
            if m + 1 < ATTN_PAIRS:
                logits_next = attention_logits(m + 1)
            tops, bots, y_off_scaled, chunk_decay = [], [], [], []
            for r0 in range(0, SLAB_HEADS, 2):
                ebs, cds = [], []
                for r in (r0, r0 + 1):
                    h = m * SLAB_HEADS + r
                    colb = jnp.broadcast_to(acs2[:, h:h + 1], (q, q))
                    row2 = row2_t[h:h + 1, :]
                    lastb = colb[q - 1:q, :]
                    l_mat = jnp.exp2(jnp.where(causal, colb - row2, -jnp.inf)) * scores
                    tops.append(l_mat.astype(BF16))
                    bots.append((b_gt * jnp.exp2(lastb - row2)).astype(BF16))
                    ebs.append(jnp.exp2(colb))
                    cds.append(jnp.exp2(lastb))
                lanes = slice(r0 * SSM_HEAD_DIM, (r0 + 2) * SSM_HEAD_DIM)
                y_off_scaled.append(y_off[:, lanes] * jnp.where(low_half, ebs[0], ebs[1]))
                chunk_decay.append(jnp.where(low_half_row, cds[0], cds[1]))
            lhs = jnp.concatenate([jnp.concatenate(tops, axis=1), jnp.concatenate(bots, axis=1)], axis=0)
            xs_slab = xbc_ref[rows, m * MXU_WIDTH:(m + 1) * MXU_WIDTH]
            rhs = jnp.concatenate(
                [jnp.where(slab_lane_head == r, xs_slab, jnp.zeros_like(xs_slab)) for r in range(SLAB_HEADS)],
                axis=0)
            yy = _dot(lhs, rhs)
            yscr[:, m * MXU_WIDTH:(m + 1) * MXU_WIDTH] = yy[0:q] + jnp.concatenate(y_off_scaled, axis=1)
            state[m] = state[m] * jnp.concatenate(chunk_decay, axis=1) + yy[q:2 * q]
            issue_pending()
            attention_output(m, logits_m)

        xs = xbc_ref[rows, 0:SSM_D_INNER].astype(F32)
        y = yscr[...] + dexp_ref[...] * xs
        yg = y * z_ref[rows, :].astype(F32)
        for g in range(SSM_GROUPS):
            gl = slice(g * SSM_GROUP_WIDTH, (g + 1) * SSM_GROUP_WIDTH)
            v = yg[:, gl]
            ms = jnp.mean(v * v, axis=-1, keepdims=True)
            ynorm[cur_rows(c), gl] = (v * lax.rsqrt(ms + RMS_EPS) * ng_ref[:, gl]).astype(BF16)

    assert not pending

    kvpad[0:ATTN_BLOCK, :] = kvpad[rows_per_step:rows_per_step + ATTN_BLOCK, :]


def _t5_bucket_matrix():
    qi = np.arange(ATTN_BLOCK)[:, None]
    kj = np.arange(2 * ATTN_BLOCK)[None, :]
    dist = np.maximum(qi + ATTN_BLOCK - kj, 0).astype(np.int32)
    max_exact = REL_BUCKETS // 2
    d = np.maximum(dist, 1).astype(np.float32)
    large = max_exact + (np.log(d / np.float32(max_exact)) / np.float32(math.log(REL_MAX_DISTANCE / max_exact))
                         * np.float32(REL_BUCKETS - max_exact)).astype(np.int32)
    large = np.minimum(large, REL_BUCKETS - 1)
    return np.where(dist < max_exact, dist, large).astype(np.int32)


def _mixer(h2d, gates, act, dt, seq, a_log, d_exp, norm_g, w_ssm, w_attn, w_out, post_g, sinks, table):
    n_tiles = h2d.shape[0] // MIX_ROWS

    def scanned(width):
        return pl.BlockSpec((MIX_ROWS, width), lambda s: (jnp.minimum(s, n_tiles - 1), 0))

    def finished(width):
        return pl.BlockSpec((MIX_ROWS, width), lambda s: (jnp.maximum(s - 1, 0), 0))

    smem = pl.BlockSpec(memory_space=pltpu.SMEM)
    bucket = jnp.asarray(_t5_bucket_matrix())
    in_specs = [
        smem, smem,
        finished(D_MODEL), finished(2 * D_MODEL),
        scanned(ACT_DIM), scanned(DT_PAD),
        _resident((1, DT_PAD)), _resident((1, SSM_D_INNER)), _resident((1, SSM_D_INNER)),
        _resident((ATTN_BLOCK, 2 * ATTN_BLOCK)),
        _resident((SSM_D_INNER, D_MODEL)), _resident((ATTN_Q_DIM, D_MODEL)), _resident((D_MODEL, D_MODEL)),
        _resident((1, D_MODEL)),
    ]
    scratch = [
        pltpu.VMEM((ATTN_BLOCK + MIX_ROWS, 2 * KV_DUP_DIM), BF16),
        pltpu.VMEM((SSM_CHUNK, SSM_D_INNER), F32),
        pltpu.VMEM((2 * MIX_ROWS, SSM_D_INNER), BF16),
        pltpu.VMEM((2 * MIX_ROWS, ATTN_Q_DIM), BF16),
        pltpu.VMEM((PROJ_ROWS, D_MODEL), F32),
        pltpu.VMEM((PROJ_ROWS, D_MODEL), BF16),
        pltpu.VMEM((PROJ_ROWS, D_MODEL), F32),
        pltpu.VMEM((SSM_SLABS, SSM_STATE, MXU_WIDTH), F32),
        pltpu.VMEM((ATTN_Q_HEADS, ATTN_BLOCK, 2 * ATTN_BLOCK), F32),
    ]
    return pl.pallas_call(
        functools.partial(_mixer_kernel, seq // MIX_ROWS),
        grid=(n_tiles + 1,),
        in_specs=in_specs,
        out_specs=finished(D_MODEL),
        out_shape=jax.ShapeDtypeStruct(h2d.shape, F32),
        scratch_shapes=scratch,
        compiler_params=pltpu.CompilerParams(dimension_semantics=("arbitrary",),
                                             vmem_limit_bytes=MIXER_VMEM_LIMIT),
        name="mixer",
    )(sinks, table, h2d, gates, act, dt, a_log, d_exp, norm_g, bucket,
      w_ssm, w_attn, w_out, post_g)


def _reorder_w_in(w_in):
    o_z = 2 * D_MODEL
    o_dt = o_z + SSM_D_INNER + SSM_CONV_DIM
    o_q = o_dt + SSM_HEADS
    o_k = o_q + ATTN_Q_DIM
    o_v = o_k + ATTN_KV_DIM

    def dup_heads(w):
        w = w.reshape(D_MODEL, ATTN_KV_HEADS, 1, ATTN_HEAD_DIM)
        return jnp.broadcast_to(w, (D_MODEL, ATTN_KV_HEADS, 2, ATTN_HEAD_DIM)).reshape(D_MODEL, KV_DUP_DIM)

    w_q = w_in[:, o_q:o_k] * (ATTN_HEAD_DIM ** -0.5 * LOG2E)
    dt_cols = jnp.pad(w_in[:, o_dt:o_q], ((0, 0), (0, DT_PAD - SSM_HEADS)))
    o_xbc = o_z + SSM_D_INNER
    parts = [w_in[:, :o_xbc] * 0.5, w_in[:, o_xbc:o_dt], w_q, dup_heads(w_in[:, o_k:o_v]),
             dup_heads(w_in[:, o_v:]), dt_cols]
    return jnp.concatenate([p.astype(BF16) for p in parts], axis=1)


def _pad_heads(v):
    return jnp.pad(v.astype(F32), (0, DT_PAD - SSM_HEADS))[None, :]


def kernel(x, ffn1_pre_g, ffn1_w_gate, ffn1_w_up, ffn1_w_down, ffn1_post_g, mix_pre_g, w_in, conv_w, conv_b, dt_bias, a_log, d_skip, ssm_norm_g, w_ssm_proj, attn_sinks, rel_bias_table, w_attn_proj, w_out, mix_post_g, ffn2_pre_g, ffn2_w_gate, ffn2_w_up, ffn2_w_down, ffn2_post_g):
    batch, seq, _ = x.shape
    depth = w_in.shape[0]
    h = x.reshape(batch * seq, D_MODEL)
    table = rel_bias_table.astype(F32) * LOG2E
    for l in range(depth):
        h = _ffn(h, ffn1_pre_g[l][None, :], (ffn1_w_gate[l] * 0.5).astype(BF16), ffn1_w_up[l].astype(BF16),
                 ffn1_w_down[l].astype(BF16), ffn1_post_g[l][None, :])
        gates, act, dt = _inproj(h, mix_pre_g[l][None, :], _reorder_w_in(w_in[l]),
                                 conv_w[l].astype(F32) * 0.5, conv_b[l].astype(F32)[None, :] * 0.5,
                                 _pad_heads(dt_bias[l]), seq)
        h = _mixer(h, gates, act, dt, seq, _pad_heads(a_log[l]),
                   jnp.repeat(d_skip[l].astype(F32), SSM_HEAD_DIM)[None, :], ssm_norm_g[l][None, :],
                   w_ssm_proj[l].astype(BF16), w_attn_proj[l].astype(BF16), w_out[l].astype(BF16),
                   mix_post_g[l][None, :], attn_sinks[l].astype(F32) * LOG2E, table)
        h = _ffn(h, ffn2_pre_g[l][None, :], (ffn2_w_gate[l] * 0.5).astype(BF16), ffn2_w_up[l].astype(BF16),
                 ffn2_w_down[l].astype(BF16), ffn2_post_g[l][None, :])
    return h.reshape(batch, seq, D_MODEL)
```

```python
import functools
import math

import numpy as np
import jax
import jax.numpy as jnp
from jax import lax
from jax.experimental import pallas as pl
from jax.experimental.pallas import tpu as pltpu

F32 = jnp.float32
BF16 = jnp.bfloat16

D_MODEL = 1024
D_FF = 2816
FFN_RESIDUAL_WEIGHT = 0.5
SSM_D_INNER = 2 * D_MODEL
SSM_HEAD_DIM = 64
SSM_HEADS = SSM_D_INNER // SSM_HEAD_DIM
SSM_GROUPS = 4
SSM_STATE = 128
SSM_CONV = 4
SSM_CHUNK = 128
SSM_BC_DIM = SSM_GROUPS * SSM_STATE
SSM_CONV_DIM = SSM_D_INNER + 2 * SSM_BC_DIM
SSM_GROUP_WIDTH = SSM_D_INNER // SSM_GROUPS
ATTN_Q_HEADS = 16
ATTN_KV_HEADS = 4
ATTN_HEAD_DIM = 64
ATTN_WINDOW = 128
ATTN_BLOCK = 128
ATTN_Q_DIM = ATTN_Q_HEADS * ATTN_HEAD_DIM
ATTN_KV_DIM = ATTN_KV_HEADS * ATTN_HEAD_DIM
REL_BUCKETS = 32
REL_MAX_DISTANCE = 128
RMS_EPS = 1e-6
LOG2E = math.log2(math.e)

LANES = 128
SUBLANES = 8
MXU_WIDTH = 256
DT_PAD = LANES
CONV_TAIL = SUBLANES
CONV_SLAB = MXU_WIDTH
CONV_BUFFERS = 4

SLAB_HEADS = MXU_WIDTH // SSM_HEAD_DIM
SSM_SLABS = SSM_D_INNER // MXU_WIDTH
SLABS_PER_GROUP = SSM_SLABS // SSM_GROUPS
ATTN_PAIRS = ATTN_Q_DIM // LANES
PAIRS_PER_KV = ATTN_PAIRS // ATTN_KV_HEADS
KV_DUP_DIM = ATTN_KV_HEADS * LANES

COL_GATES = 0
COL_Z = COL_GATES + 2 * D_MODEL
COL_XBC = COL_Z + SSM_D_INNER
COL_QKV = COL_XBC + SSM_CONV_DIM
QKV_DIM = ATTN_Q_DIM + 2 * KV_DUP_DIM
COL_DT = COL_QKV + QKV_DIM
IN_COLS_PADDED = COL_DT + DT_PAD
ACT_DIM = SSM_D_INNER + SSM_CONV_DIM + QKV_DIM

FFN_ROWS = 1024
FFN_SUB_ROWS = 512
INPROJ_ROWS = 512
MIX_ROWS = 512
PROJ_ROWS = 256
VMEM_LIMIT = 56 * 1024 * 1024
MIXER_VMEM_LIMIT = 60 * 1024 * 1024


def _rms(x, g):
    ms = jnp.mean(x * x, axis=-1, keepdims=True)
    return x * lax.rsqrt(ms + RMS_EPS) * g


def _twice_sigmoid_of_half(xh):
    return jnp.tanh(xh) + 1.0


def _silu_of_half(xh):
    return xh * (jnp.tanh(xh) + 1.0)


def _softplus(x):
    return jnp.maximum(x, 0.0) + jnp.log1p(jnp.exp(-jnp.abs(x)))


def _dot(a, b):
    return jnp.dot(a, b, preferred_element_type=F32)


def _dot_nt(a, b):
    return lax.dot_general(a, b, (((1,), (1,)), ((), ())), preferred_element_type=F32)


def _split3(x):
    hi = x.astype(BF16)
    r1 = x - hi.astype(F32)
    mid = r1.astype(BF16)
    lo = (r1 - mid.astype(F32)).astype(BF16)
    return hi, mid, lo


def _split_act(act_ref):
    return (act_ref.at[:, 0:SSM_D_INNER], act_ref.at[:, SSM_D_INNER:SSM_D_INNER + SSM_CONV_DIM],
            act_ref.at[:, SSM_D_INNER + SSM_CONV_DIM:ACT_DIM])


def _resident(shape):
    return pl.BlockSpec(shape, lambda *_: (0,) * len(shape), pipeline_mode=pl.Buffered(1))


def _ffn_kernel(x_ref, pre_ref, wg_ref, wu_ref, wd_ref, post_ref, o_ref):
    for r0 in range(0, x_ref.shape[0], FFN_SUB_ROWS):
        rows = slice(r0, r0 + FFN_SUB_ROWS)
        x = x_ref[rows, :]
        u = _rms(x, pre_ref[...]).astype(BF16)
        g = _dot(u, wg_ref[...])
        up = _dot(u, wu_ref[...])
        hmid = (_silu_of_half(g) * up).astype(BF16)
        f = _dot(hmid, wd_ref[...])
        o_ref[rows, :] = x + _rms(f, post_ref[...])


def _ffn(x2d, pre_g, wg, wu, wd, post_g):
    t = x2d.shape[0]
    row = pl.BlockSpec((FFN_ROWS, D_MODEL), lambda i: (i, 0))
    return pl.pallas_call(
        _ffn_kernel,
        grid=(t // FFN_ROWS,),
        in_specs=[row, _resident((1, D_MODEL)), _resident((D_MODEL, D_FF)), _resident((D_MODEL, D_FF)),
                  _resident((D_FF, D_MODEL)), _resident((1, D_MODEL))],
        out_specs=row,
        out_shape=jax.ShapeDtypeStruct((t, D_MODEL), F32),
        compiler_params=pltpu.CompilerParams(dimension_semantics=("arbitrary",),
                                             vmem_limit_bytes=VMEM_LIMIT),
        name="ffn",
    )(x2d, pre_g, wg, wu, wd, post_g)


def _inproj_kernel(seq, h_ref, g_ref, w_ref, convw_ref, convb_ref, dtb_ref,
                   gates_o, act_o, dt_o, xpad, tail):
    rows = h_ref.shape[0]
    i = pl.program_id(0)
    z_o, xbc_o, qkv_o = _split_act(act_o)

    @pl.when((i * rows) % seq == 0)
    def _new_sequence():
        tail[...] = jnp.zeros_like(tail)

    u = _rms(h_ref[...], g_ref[...]).astype(BF16)

    def conv_slab(n):
        cols = slice(n * CONV_SLAB, (n + 1) * CONV_SLAB)
        buf = xpad.at[n % CONV_BUFFERS]
        buf[0:CONV_TAIL, :] = tail[:, cols]
        buf[CONV_TAIL:CONV_TAIL + rows, :] = _dot(u, w_ref[:, COL_XBC + n * CONV_SLAB:COL_XBC + (n + 1) * CONV_SLAB])
        conv = convb_ref[:, cols]
        for k in range(SSM_CONV):
            start = CONV_TAIL - (SSM_CONV - 1) + k
            conv = conv + convw_ref[k:k + 1, cols] * buf[start:start + rows, :]
        xbc_o[:, cols] = _silu_of_half(conv).astype(BF16)
        tail[:, cols] = buf[rows:rows + CONV_TAIL, :]

    def piece(out_ref, col0, n, act):
        cols = slice(n * CONV_SLAB, (n + 1) * CONV_SLAB)
        out_ref[:, cols] = act(_dot(u, w_ref[:, col0 + n * CONV_SLAB:col0 + (n + 1) * CONV_SLAB])).astype(BF16)

    def dt_piece():
        dt_o[...] = _softplus(_dot(u, w_ref[:, COL_DT:IN_COLS_PADDED]) + dtb_ref[...])

    others = []
    for n in range(QKV_DIM // CONV_SLAB):
        others.append(functools.partial(piece, gates_o, COL_GATES, n, _twice_sigmoid_of_half))
        others.append(functools.partial(piece, qkv_o, COL_QKV, n, lambda v: v))
        others.append(functools.partial(piece, z_o, COL_Z, n, _silu_of_half))
    others.append(dt_piece)
    n_conv = SSM_CONV_DIM // CONV_SLAB
    for n in range(n_conv):
        conv_slab(n)
        for k in range(n * len(others) // n_conv, (n + 1) * len(others) // n_conv):
            others[k]()


def _inproj(h2d, g, w, conv_w, conv_b, dt_bias, seq):
    t = h2d.shape[0]

    def row(width):
        return pl.BlockSpec((INPROJ_ROWS, width), lambda i: (i, 0))

    widths = (2 * D_MODEL, ACT_DIM, DT_PAD)
    dtypes = (BF16, BF16, F32)
    return pl.pallas_call(
        functools.partial(_inproj_kernel, seq),
        grid=(t // INPROJ_ROWS,),
        in_specs=[row(D_MODEL), _resident((1, D_MODEL)), _resident((D_MODEL, IN_COLS_PADDED)),
                  _resident((SSM_CONV, SSM_CONV_DIM)), _resident((1, SSM_CONV_DIM)), _resident((1, DT_PAD))],
        out_specs=[row(w_) for w_ in widths],
        out_shape=[jax.ShapeDtypeStruct((t, w_), d_) for w_, d_ in zip(widths, dtypes)],
        scratch_shapes=[pltpu.VMEM((CONV_BUFFERS, CONV_TAIL + INPROJ_ROWS, CONV_SLAB), F32),
                        pltpu.VMEM((CONV_TAIL, SSM_CONV_DIM), F32)],
        compiler_params=pltpu.CompilerParams(dimension_semantics=("arbitrary",),
                                             vmem_limit_bytes=VMEM_LIMIT),
        name="inproj",
    )(h2d, g, w, conv_w, conv_b, dt_bias)


def _mixer_kernel(tiles_per_seq, sinks_ref, table_ref,
                  h_ref, gates_ref, act_ref, dt_ref,
                  alog_ref, dexp_ref, ng_ref, bucket_ref,
                  wssm_ref, wattn_ref, wout_ref, postg_ref,
                  o_ref,
                  kvpad, yscr, ynorm, attn, pssm, pmerged, pmix, state, bias):
    z_ref, xbc_ref, qkv_ref = _split_act(act_ref)
    rows_per_step = act_ref.shape[0]
    n_chunks = rows_per_step // SSM_CHUNK
    s = pl.program_id(0)
    q = SSM_CHUNK
    cur_base = (s % 2) * rows_per_step
    prev_base = rows_per_step - cur_base

    def cur_rows(c):
        return pl.ds(pl.multiple_of(cur_base + c * q, q), q)


    ii = lax.broadcasted_iota(jnp.int32, (q, q), 0)
    jj = lax.broadcasted_iota(jnp.int32, (q, q), 1)
    causal = ii >= jj
    tril3 = jnp.concatenate([jnp.where(causal, 1.0, 0.0).astype(BF16)] * 3, axis=1)
    low_half = jj < SSM_HEAD_DIM
    low_half_row = low_half[0:1, :]
    slab_lane_head = lax.broadcasted_iota(jnp.int32, (q, MXU_WIDTH), 1) // SSM_HEAD_DIM
    band_low_half = lax.broadcasted_iota(jnp.int32, (2 * ATTN_BLOCK, LANES), 1) < ATTN_HEAD_DIM

    @pl.when(s == 0)
    def _init():
        bi = lax.broadcasted_iota(jnp.int32, (ATTN_BLOCK, 2 * ATTN_BLOCK), 0)
        bj = lax.broadcasted_iota(jnp.int32, (ATTN_BLOCK, 2 * ATTN_BLOCK), 1)
        dist = bi + ATTN_BLOCK - bj
        in_window = (dist >= 0) & (dist < ATTN_WINDOW)
        bucket = bucket_ref[...]
        for hq in range(ATTN_Q_HEADS):
            acc = jnp.zeros((ATTN_BLOCK, 2 * ATTN_BLOCK), F32)
            for bk in range(REL_BUCKETS):
                acc = jnp.where(bucket == bk, table_ref[bk, hq], acc)
            bias[hq] = jnp.where(in_window, acc, -jnp.inf)
        ynorm[rows_per_step:2 * rows_per_step, :] = jnp.zeros((rows_per_step, SSM_D_INNER), BF16)
        attn[rows_per_step:2 * rows_per_step, :] = jnp.zeros((rows_per_step, ATTN_Q_DIM), BF16)

    new_sequence = s % tiles_per_seq == 0

    @pl.when(new_sequence)
    def _reset_sequence_state():
        state[...] = jnp.zeros_like(state)
        kvpad[0:ATTN_BLOCK, :] = jnp.zeros((ATTN_BLOCK, 2 * KV_DUP_DIM), BF16)

    kvpad[ATTN_BLOCK:ATTN_BLOCK + rows_per_step, :] = qkv_ref[:, ATTN_Q_DIM:QKV_DIM]

    a_neg = -jnp.exp(alog_ref[...])

    pending = []
    for blk in range(rows_per_step // PROJ_ROWS):
        brows = slice(blk * PROJ_ROWS, (blk + 1) * PROJ_ROWS)
        prows = pl.ds(pl.multiple_of(prev_base + blk * PROJ_ROWS, PROJ_ROWS), PROJ_ROWS)
        for n in range(D_MODEL // MXU_WIDTH):
            ns = slice(n * MXU_WIDTH, (n + 1) * MXU_WIDTH)
            ns_attn = slice(D_MODEL + n * MXU_WIDTH, D_MODEL + (n + 1) * MXU_WIDTH)

            def ssm_item(ns=ns, prows=prows):
                pssm[:, ns] = _dot(ynorm[prows, :], wssm_ref[:, ns])

            def attn_merge_item(ns=ns, ns_attn=ns_attn, prows=prows, brows=brows):
                y_attn = _dot(attn[prows, :], wattn_ref[:, ns])
                merged = (gates_ref[brows, ns].astype(F32) * pssm[:, ns]
                          + gates_ref[brows, ns_attn].astype(F32) * y_attn)
                pmerged[:, ns] = merged.astype(BF16)

            pending += [ssm_item, attn_merge_item]
        for n in range(D_MODEL // MXU_WIDTH):
            ns = slice(n * MXU_WIDTH, (n + 1) * MXU_WIDTH)

            def out_item(ns=ns):
                pmix[:, ns] = _dot(pmerged[...], wout_ref[:, ns])

            pending.append(out_item)

        def residual_item(brows=brows):
            o_ref[brows, :] = h_ref[brows, :] + _rms(pmix[...], postg_ref[...])

        pending.append(residual_item)
    n_slots = n_chunks * 2 * SSM_SLABS
    assert len(pending) <= n_slots
    issue_at = {(k * n_slots) // len(pending) for k in range(len(pending))}
    slot = [0]

    def issue_pending():
        if slot[0] in issue_at:
            pending.pop(0)()
        slot[0] += 1

    for c in range(n_chunks):
        rows = slice(c * q, (c + 1) * q)

        dt = dt_ref[rows, :]
        hi, mid, lo = _split3(dt * a_neg)
        acs2 = _dot(tril3, jnp.concatenate([hi, mid, lo], axis=0)) * LOG2E
        row2_t = (acs2 - jnp.log2(dt)).T

        band = slice(c * q, c * q + 2 * ATTN_BLOCK)
        if c == 0:
            key_idx = lax.broadcasted_iota(jnp.int32, (ATTN_BLOCK, 2 * ATTN_BLOCK), 1)
            no_prev_block = jnp.where(new_sequence & (key_idx < ATTN_BLOCK), -jnp.inf, 0.0)

        def attention_logits(pair):
            kv = pair // PAIRS_PER_KV
            kd = kvpad[band, kv * LANES:(kv + 1) * LANES]
            zero = jnp.zeros_like(kd)
            k_cat = jnp.concatenate([jnp.where(band_low_half, kd, zero), jnp.where(band_low_half, zero, kd)], axis=0)
            q_pair = qkv_ref[rows, pair * LANES:(pair + 1) * LANES]
            return _dot_nt(q_pair, k_cat)

        def attention_output(pair, logits):
            kv = pair // PAIRS_PER_KV
            vd = kvpad[band, KV_DUP_DIM + kv * LANES:KV_DUP_DIM + (kv + 1) * LANES]
            zero = jnp.zeros_like(vd)
            v_cat = jnp.concatenate([jnp.where(band_low_half, vd, zero), jnp.where(band_low_half, zero, vd)], axis=0)
            ps, invs = [], []
            for half in range(2):
                hq = 2 * pair + half
                l2 = logits[:, half * 2 * ATTN_BLOCK:(half + 1) * 2 * ATTN_BLOCK] + bias[hq]
                if c == 0:
                    l2 = l2 + no_prev_block
                sink = sinks_ref[hq]
                mx = jnp.maximum(jnp.max(l2, axis=-1, keepdims=True), sink)
                p = jnp.exp2(l2 - mx)
                den = jnp.sum(p, axis=-1, keepdims=True) + jnp.exp2(sink - mx)
                ps.append(p.astype(BF16))
                invs.append(1.0 / den)
            o = _dot(jnp.concatenate(ps, axis=1), v_cat)
            o = o * jnp.where(low_half, invs[0], invs[1])
            attn[cur_rows(c), pair * LANES:(pair + 1) * LANES] = o.astype(BF16)

        logits_next = attention_logits(0)
        for m in range(SSM_SLABS):
            logits_m = logits_next
            issue_pending()
            g = m // SLABS_PER_GROUP
            if m % SLABS_PER_GROUP == 0:
                b_g = xbc_ref[rows, SSM_D_INNER + g * SSM_STATE:SSM_D_INNER + (g + 1) * SSM_STATE]
                c_g = xbc_ref[rows, SSM_D_INNER + SSM_BC_DIM + g * SSM_STATE:
                              SSM_D_INNER + SSM_BC_DIM + (g + 1) * SSM_STATE]
                scores = _dot_nt(c_g, b_g)
                b_gt = b_g.astype(F32).T
            y_off = _dot(c_g, state[m].astype(BF16))
            if m + 1 < ATTN_PAIRS:
                logits_next = attention_logits(m + 1)
            tops, bots, y_off_scaled, chunk_decay = [], [], [], []
            for r0 in range(0, SLAB_HEADS, 2):
                ebs, cds = [], []
                for r in (r0, r0 + 1):
                    h = m * SLAB_HEADS + r
                    colb = jnp.broadcast_to(acs2[:, h:h + 1], (q, q))
                    row2 = row2_t[h:h + 1, :]
                    lastb = colb[q - 1:q, :]
                    l_mat = jnp.exp2(jnp.where(causal, colb - row2, -jnp.inf)) * scores
                    tops.append(l_mat.astype(BF16))
                    bots.append((b_gt * jnp.exp2(lastb - row2)).astype(BF16))
                    ebs.append(jnp.exp2(colb))
                    cds.append(jnp.exp2(lastb))
                lanes = slice(r0 * SSM_HEAD_DIM, (r0 + 2) * SSM_HEAD_DIM)
                y_off_scaled.append(y_off[:, lanes] * jnp.where(low_half, ebs[0], ebs[1]))
                chunk_decay.append(jnp.where(low_half_row, cds[0], cds[1]))
            lhs = jnp.concatenate([jnp.concatenate(tops, axis=1), jnp.concatenate(bots, axis=1)], axis=0)
            xs_slab = xbc_ref[rows, m * MXU_WIDTH:(m + 1) * MXU_WIDTH]
            rhs = jnp.concatenate(
                [jnp.where(slab_lane_head == r, xs_slab, jnp.zeros_like(xs_slab)) for r in range(SLAB_HEADS)],
                axis=0)
            yy = _dot(lhs, rhs)
            yscr[:, m * MXU_WIDTH:(m + 1) * MXU_WIDTH] = yy[0:q] + jnp.concatenate(y_off_scaled, axis=1)
            state[m] = state[m] * jnp.concatenate(chunk_decay, axis=1) + yy[q:2 * q]
            issue_pending()
            attention_output(m, logits_m)

        xs = xbc_ref[rows, 0:SSM_D_INNER].astype(F32)
        y = yscr[...] + dexp_ref[...] * xs
        yg = y * z_ref[rows, :].astype(F32)
        for g in range(SSM_GROUPS):
            gl = slice(g * SSM_GROUP_WIDTH, (g + 1) * SSM_GROUP_WIDTH)
            v = yg[:, gl]
            ms = jnp.mean(v * v, axis=-1, keepdims=True)
            ynorm[cur_rows(c), gl] = (v * lax.rsqrt(ms + RMS_EPS) * ng_ref[:, gl]).astype(BF16)

    assert not pending

    kvpad[0:ATTN_BLOCK, :] = kvpad[rows_per_step:rows_per_step + ATTN_BLOCK, :]


def _t5_bucket_matrix():
    qi = np.arange(ATTN_BLOCK)[:, None]
    kj = np.arange(2 * ATTN_BLOCK)[None, :]
    dist = np.maximum(qi + ATTN_BLOCK - kj, 0).astype(np.int32)
    max_exact = REL_BUCKETS // 2
    d = np.maximum(dist, 1).astype(np.float32)
    large = max_exact + (np.log(d / np.float32(max_exact)) / np.float32(math.log(REL_MAX_DISTANCE / max_exact))
                         * np.float32(REL_BUCKETS - max_exact)).astype(np.int32)
    large = np.minimum(large, REL_BUCKETS - 1)
    return np.where(dist < max_exact, dist, large).astype(np.int32)


def _mixer(h2d, gates, act, dt, seq, a_log, d_exp, norm_g, w_ssm, w_attn, w_out, post_g, sinks, table):
    n_tiles = h2d.shape[0] // MIX_ROWS

    def scanned(width):
        return pl.BlockSpec((MIX_ROWS, width), lambda s: (jnp.minimum(s, n_tiles - 1), 0))

    def finished(width):
        return pl.BlockSpec((MIX_ROWS, width), lambda s: (jnp.maximum(s - 1, 0), 0))

    smem = pl.BlockSpec(memory_space=pltpu.SMEM)
    bucket = jnp.asarray(_t5_bucket_matrix())
    in_specs = [
        smem, smem,
        finished(D_MODEL), finished(2 * D_MODEL),
        scanned(ACT_DIM), scanned(DT_PAD),
        _resident((1, DT_PAD)), _resident((1, SSM_D_INNER)), _resident((1, SSM_D_INNER)),
        _resident((ATTN_BLOCK, 2 * ATTN_BLOCK)),
        _resident((SSM_D_INNER, D_MODEL)), _resident((ATTN_Q_DIM, D_MODEL)), _resident((D_MODEL, D_MODEL)),
        _resident((1, D_MODEL)),
    ]
    scratch = [
        pltpu.VMEM((ATTN_BLOCK + MIX_ROWS, 2 * KV_DUP_DIM), BF16),
        pltpu.VMEM((SSM_CHUNK, SSM_D_INNER), F32),
        pltpu.VMEM((2 * MIX_ROWS, SSM_D_INNER), BF16),
        pltpu.VMEM((2 * MIX_ROWS, ATTN_Q_DIM), BF16),
        pltpu.VMEM((PROJ_ROWS, D_MODEL), F32),
        pltpu.VMEM((PROJ_ROWS, D_MODEL), BF16),
        pltpu.VMEM((PROJ_ROWS, D_MODEL), F32),
        pltpu.VMEM((SSM_SLABS, SSM_STATE, MXU_WIDTH), F32),
        pltpu.VMEM((ATTN_Q_HEADS, ATTN_BLOCK, 2 * ATTN_BLOCK), F32),
    ]
    return pl.pallas_call(
        functools.partial(_mixer_kernel, seq // MIX_ROWS),
        grid=(n_tiles + 1,),
        in_specs=in_specs,
        out_specs=finished(D_MODEL),
        out_shape=jax.ShapeDtypeStruct(h2d.shape, F32),
        scratch_shapes=scratch,
        compiler_params=pltpu.CompilerParams(dimension_semantics=("arbitrary",),
                                             vmem_limit_bytes=MIXER_VMEM_LIMIT),
        name="mixer",
    )(sinks, table, h2d, gates, act, dt, a_log, d_exp, norm_g, bucket,
      w_ssm, w_attn, w_out, post_g)


def _reorder_w_in(w_in):
    o_z = 2 * D_MODEL
    o_dt = o_z + SSM_D_INNER + SSM_CONV_DIM
    o_q = o_dt + SSM_HEADS
    o_k = o_q + ATTN_Q_DIM
    o_v = o_k + ATTN_KV_DIM

    def dup_heads(w):
        w = w.reshape(D_MODEL, ATTN_KV_HEADS, 1, ATTN_HEAD_DIM)
        return jnp.broadcast_to(w, (D_MODEL, ATTN_KV_HEADS, 2, ATTN_HEAD_DIM)).reshape(D_MODEL, KV_DUP_DIM)

    w_q = w_in[:, o_q:o_k] * (ATTN_HEAD_DIM ** -0.5 * LOG2E)
    dt_cols = jnp.pad(w_in[:, o_dt:o_q], ((0, 0), (0, DT_PAD - SSM_HEADS)))
    o_xbc = o_z + SSM_D_INNER
    parts = [w_in[:, :o_xbc] * 0.5, w_in[:, o_xbc:o_dt], w_q, dup_heads(w_in[:, o_k:o_v]),
             dup_heads(w_in[:, o_v:]), dt_cols]
    return jnp.concatenate([p.astype(BF16) for p in parts], axis=1)


def _pad_heads(v):
    return jnp.pad(v.astype(F32), (0, DT_PAD - SSM_HEADS))[None, :]


def kernel(x, ffn1_pre_g, ffn1_w_gate, ffn1_w_up, ffn1_w_down, ffn1_post_g, mix_pre_g, w_in, conv_w, conv_b, dt_bias, a_log, d_skip, ssm_norm_g, w_ssm_proj, attn_sinks, rel_bias_table, w_attn_proj, w_out, mix_post_g, ffn2_pre_g, ffn2_w_gate, ffn2_w_up, ffn2_w_down, ffn2_post_g):
    batch, seq, _ = x.shape
    depth = w_in.shape[0]
    h = x.reshape(batch * seq, D_MODEL)
    table = rel_bias_table.astype(F32) * LOG2E
    for l in range(depth):
        h = _ffn(h, ffn1_pre_g[l][None, :], (ffn1_w_gate[l] * 0.5).astype(BF16), ffn1_w_up[l].astype(BF16),
                 ffn1_w_down[l].astype(BF16), ffn1_post_g[l].astype(F32)[None, :] * FFN_RESIDUAL_WEIGHT)
        gates, act, dt = _inproj(h, mix_pre_g[l][None, :], _reorder_w_in(w_in[l]),
                                 conv_w[l].astype(F32) * 0.5, conv_b[l].astype(F32)[None, :] * 0.5,
                                 _pad_heads(dt_bias[l]), seq)
        h = _mixer(h, gates, act, dt, seq, _pad_heads(a_log[l]),
                   jnp.repeat(d_skip[l].astype(F32), SSM_HEAD_DIM)[None, :], ssm_norm_g[l][None, :],
                   w_ssm_proj[l].astype(BF16), w_attn_proj[l].astype(BF16), (w_out[l] * 0.5).astype(BF16),
                   mix_post_g[l][None, :], attn_sinks[l].astype(F32) * LOG2E, table)
        h = _ffn(h, ffn2_pre_g[l][None, :], (ffn2_w_gate[l] * 0.5).astype(BF16), ffn2_w_up[l].astype(BF16),
                 ffn2_w_down[l].astype(BF16), ffn2_post_g[l].astype(F32)[None, :] * FFN_RESIDUAL_WEIGHT)
    return h.reshape(batch, seq, D_MODEL)
```
